```python
import jax, jax.numpy as jnp
from jax import lax
import numpy as np

D_MODEL = 2048
BATCH = 2
SEQ = 16384
DEPTH = 2

GRID_W = 64
CTX_LEN = 256
HEAD_DIM = 128
N_HEAD_SLOTS = D_MODEL // HEAD_DIM
NA_HEADS = N_HEAD_SLOTS // 4
NA_KH = 8
NA_KW = 16
GLA_HEADS = N_HEAD_SLOTS // 4
GLA_DV = HEAD_DIM
GLA_DK = HEAD_DIM // 2
GLA_GATE_RANK = 16
GLA_GATE_NORM = 16.0
GLA_CHUNK = 64
SWA_Q_HEADS = N_HEAD_SLOTS // 2
SWA_KV_HEADS = SWA_Q_HEADS // 4
SWA_WINDOW = 128
SWA_BLOCK = 128
ROPE_THETA = 10000.0
ROPE_AXIS_DIM = HEAD_DIM // 2
NA_W = NA_HEADS * HEAD_DIM
GLA_QK_W = GLA_HEADS * GLA_DK
GLA_V_W = GLA_HEADS * GLA_DV
SWA_Q_W = SWA_Q_HEADS * HEAD_DIM
SWA_KV_W = SWA_KV_HEADS * HEAD_DIM
D_MIX = NA_W + GLA_V_W + SWA_Q_W
IN_SPLITS = (NA_W, NA_W, NA_W,
             GLA_QK_W, GLA_QK_W, GLA_V_W, GLA_V_W, GLA_GATE_RANK, GLA_GATE_RANK,
             SWA_Q_W, SWA_KV_W, SWA_KV_W)
D_IN = sum(IN_SPLITS)
D_FF = 4 * D_MODEL
EPS = 1e-6
NEG_INF = -1e30

kernel_name = "hybrid_natten_gla_swa_dit_block"


def _rmsnorm(x, g):
    xf = x.astype(jnp.float32)
    y = xf * lax.rsqrt(jnp.mean(xf * xf, axis=-1, keepdims=True) + EPS)
    return (y * g.astype(jnp.float32)).astype(x.dtype)


def _heads(t, h):
    return t.reshape(t.shape[:-1] + (h, t.shape[-1] // h))


def _split_cols(z):
    idx = [int(v) for v in np.cumsum(IN_SPLITS)[:-1]]
    return jnp.split(z, idx, axis=-1)


def _axial_rope(L):
    t = jnp.arange(L, dtype=jnp.int32)
    row = (t // GRID_W).astype(jnp.float32)
    col = (t % GRID_W).astype(jnp.float32)
    n_freq = ROPE_AXIS_DIM // 2
    inv = ROPE_THETA ** (-jnp.arange(n_freq, dtype=jnp.float32) / n_freq)
    ar = row[:, None] * inv[None]
    ac = col[:, None] * inv[None]
    return (jnp.cos(ar)[:, None, :], jnp.sin(ar)[:, None, :],
            jnp.cos(ac)[:, None, :], jnp.sin(ac)[:, None, :])


def _apply_rope(x, tabs):
    cr, sr, cc, sc = tabs
    xf = x.astype(jnp.float32)
    xr, xcol = jnp.split(xf, 2, axis=-1)

    def rot(u, cos, sin):
        u1, u2 = jnp.split(u, 2, axis=-1)
        return jnp.concatenate([u1 * cos - u2 * sin, u2 * cos + u1 * sin], axis=-1)

    return jnp.concatenate([rot(xr, cr, sr), rot(xcol, cc, sc)], axis=-1).astype(x.dtype)


def _dense_attention(q, k, v, sink):
    B, M, Hq, hd = q.shape
    Hkv = k.shape[2]
    G = Hq // Hkv
    qg = q.reshape(B, M, Hkv, G, hd)
    s = jnp.einsum('bqhgd,bkhd->bhgqk', qg, k, preferred_element_type=jnp.float32) * (hd ** -0.5)
    if sink is not None:
        snk = jnp.broadcast_to(sink.astype(jnp.float32).reshape(1, Hkv, G, 1, 1), s.shape[:-1] + (1,))
        s = jnp.concatenate([s, snk], axis=-1)
    p = jax.nn.softmax(s, axis=-1)
    if sink is not None:
        p = p[..., :-1]
    o = jnp.einsum('bhgqk,bkhd->bqhgd', p.astype(v.dtype), v)
    return o.reshape(B, M, Hq, hd)


def _neighbourhood_attention(q, k, v, kc, vc, rpb):
    B, L, H, hd = q.shape
    rows = L // GRID_W
    kh = min(NA_KH, rows)
    scale = hd ** -0.5
    qg = q.reshape(B, rows, GRID_W, H, hd).transpose(1, 0, 2, 3, 4)
    kg = k.reshape(B, rows, GRID_W, H, hd)
    vg = v.reshape(B, rows, GRID_W, H, hd)
    cols = np.arange(GRID_W)
    col_start = np.clip(cols - NA_KW // 2, 0, GRID_W - NA_KW)
    col_idx = col_start[:, None] + np.arange(NA_KW)[None, :]
    col_bias_idx = col_idx - cols[:, None] + NA_KW - 1
    n_loc = kh * NA_KW

    def one_row(args):
        r, q_row = args
        rs = jnp.clip(r - kh // 2, 0, rows - kh)
        k_rows = lax.dynamic_slice_in_dim(kg, rs, kh, axis=1)
        v_rows = lax.dynamic_slice_in_dim(vg, rs, kh, axis=1)
        k_win = k_rows[:, :, col_idx]
        v_win = v_rows[:, :, col_idx]
        s_loc = jnp.einsum('bchd,bicjhd->bhcij', q_row, k_win, preferred_element_type=jnp.float32) * scale
        row_bias_idx = rs + jnp.arange(kh) - r + NA_KH - 1
        bias = rpb[:, row_bias_idx[None, :, None], col_bias_idx[:, None, :]]
        s_loc = (s_loc + bias[None].astype(jnp.float32)).reshape(B, H, GRID_W, n_loc)
        s_ctx = jnp.einsum('bchd,bmhd->bhcm', q_row, kc, preferred_element_type=jnp.float32) * scale
        p = jax.nn.softmax(jnp.concatenate([s_loc, s_ctx], axis=-1), axis=-1).astype(v.dtype)
        p_loc = p[..., :n_loc].reshape(B, H, GRID_W, kh, NA_KW)
        p_ctx = p[..., n_loc:]
        return (jnp.einsum('bhcij,bicjhd->bchd', p_loc, v_win)
                + jnp.einsum('bhcm,bmhd->bchd', p_ctx, vc))

    out = lax.map(one_row, (jnp.arange(rows), qg))
    return out.transpose(1, 0, 2, 3, 4).reshape(B, L, H, hd)


def _window_attention(q, k, v, kc, vc, sink):
    B, L, Hq, hd = q.shape
    Hkv = k.shape[2]
    G = Hq // Hkv
    nb = L // SWA_BLOCK
    band = 3 * SWA_BLOCK
    scale = hd ** -0.5
    pad = ((0, 0), (SWA_BLOCK, SWA_BLOCK), (0, 0), (0, 0))
    kp = jnp.pad(k, pad)
    vp = jnp.pad(v, pad)
    qb = q.reshape(B, nb, SWA_BLOCK, Hkv, G, hd).transpose(1, 0, 2, 3, 4, 5)
    rel = jnp.arange(band)[None, :] - SWA_BLOCK - jnp.arange(SWA_BLOCK)[:, None]
    in_window = jnp.abs(rel) <= SWA_WINDOW
    sink_logit = sink.astype(jnp.float32).reshape(1, Hkv, G, 1, 1)

    def one_block(args):
        n, q_blk = args
        start = n * SWA_BLOCK
        k_blk = lax.dynamic_slice_in_dim(kp, start, band, axis=1)
        v_blk = lax.dynamic_slice_in_dim(vp, start, band, axis=1)
        kpos = start - SWA_BLOCK + jnp.arange(band)
        valid = in_window & ((kpos >= 0) & (kpos < L))[None, :]
        s_loc = jnp.einsum('bqhgd,bkhd->bhgqk', q_blk, k_blk, preferred_element_type=jnp.float32) * scale
        s_loc = jnp.where(valid, s_loc, NEG_INF)
        s_ctx = jnp.einsum('bqhgd,bmhd->bhgqm', q_blk, kc, preferred_element_type=jnp.float32) * scale
        s_snk = jnp.broadcast_to(sink_logit, (B, Hkv, G, SWA_BLOCK, 1))
        p = jax.nn.softmax(jnp.concatenate([s_loc, s_ctx, s_snk], axis=-1), axis=-1).astype(v.dtype)
        p_loc = p[..., :band]
        p_ctx = p[..., band:band + kc.shape[1]]
        return (jnp.einsum('bhgqk,bkhd->bqhgd', p_loc, v_blk)
                + jnp.einsum('bhgqm,bmhd->bqhgd', p_ctx, vc))

    out = lax.map(one_block, (jnp.arange(nb), qb))
    return out.transpose(1, 0, 2, 3, 4, 5).reshape(B, L, Hq, hd)


def _gla_chunk_scan(q, k, v, g, s0):
    B, L, H, dk = q.shape
    dv = v.shape[-1]
    C = GLA_CHUNK
    n = L // C

    def to_chunks(t):
        return t.astype(jnp.float32).reshape(B, n, C, H, t.shape[-1]).transpose(1, 0, 3, 2, 4)

    causal = jnp.tril(jnp.ones((C, C), dtype=bool))[None, None, :, :, None]

    def step(S, inp):
        qc, kc, vc, gc = inp
        b = jnp.cumsum(gc, axis=2)
        o_inter = jnp.einsum('bhik,bhkv->bhiv', qc * jnp.exp(b), S)
        diff = b[:, :, :, None, :] - b[:, :, None, :, :]
        decay = jnp.exp(jnp.where(causal, diff, -jnp.inf))
        a = jnp.einsum('bhik,bhjk,bhijk->bhij', qc, kc, decay)
        o = o_inter + jnp.einsum('bhij,bhjv->bhiv', a, vc)
        b_last = b[:, :, -1:, :]
        S_new = (jnp.exp(b_last[:, :, 0, :])[..., None] * S
                 + jnp.einsum('bhjk,bhjv->bhkv', kc * jnp.exp(b_last - b), vc))
        return S_new, o

    S_fin, o = lax.scan(step, s0, (to_chunks(q), to_chunks(k), to_chunks(v), to_chunks(g)))
    o = o.transpose(1, 0, 3, 2, 4).reshape(B, L, H, dv)
    return o, S_fin


def _gla_bidirectional(q, k, v, gf, gb, q_c, k_c, v_c, gf_c, gb_c):
    B, _, H, dk = q.shape
    dv = v.shape[-1]
    s0 = jnp.zeros((B, H, dk, dv), jnp.float32)
    fl = lambda t: jnp.flip(t, axis=1)
    oc_f, sc_f = _gla_chunk_scan(q_c, k_c, v_c, gf_c, s0)
    oc_b, sc_b = _gla_chunk_scan(fl(q_c), fl(k_c), fl(v_c), fl(gb_c), s0)
    ol_f, _ = _gla_chunk_scan(q, k, v, gf, sc_f)
    ol_b, _ = _gla_chunk_scan(fl(q), fl(k), fl(v), fl(gb), sc_b)
    return ol_f + fl(ol_b), oc_f + fl(oc_b)


def _gla_output(o, r, g):
    of = o.astype(jnp.float32)
    of = of * lax.rsqrt(jnp.mean(of * of, axis=-1, keepdims=True) + EPS) * g.astype(jnp.float32)
    of = of.reshape(r.shape)
    return (of * jax.nn.silu(r.astype(jnp.float32))).astype(r.dtype)


def _squared_relu_mlp(h, w1, w2):
    return jnp.square(jax.nn.relu(h @ w1)) @ w2


def _mixers(z, zc, rpb, wg_f, bg_f, wg_b, bg_b, gla_g, sink, rope, with_ctx):
    qa, ka, va, qg, kg, vg, rg, gfl, gbl, qs, ks, vs = _split_cols(z)
    qa_c, ka_c, va_c, qg_c, kg_c, vg_c, rg_c, gfl_c, gbl_c, qs_c, ks_c, vs_c = _split_cols(zc)
    B, L = z.shape[0], z.shape[1]
    M = zc.shape[1]
    ka_ch, va_ch = _heads(ka_c, NA_HEADS), _heads(va_c, NA_HEADS)
    ya = _neighbourhood_attention(_heads(qa, NA_HEADS), _heads(ka, NA_HEADS), _heads(va, NA_HEADS),
                                  ka_ch, va_ch, rpb)
    gate = lambda u, w, b: _heads(jax.nn.log_sigmoid((u @ w + b).astype(jnp.float32)) / GLA_GATE_NORM, GLA_HEADS)
    qscale = GLA_DK ** -0.5
    ob, ob_c = _gla_bidirectional(
        _heads(qg, GLA_HEADS) * qscale, _heads(kg, GLA_HEADS), _heads(vg, GLA_HEADS),
        gate(gfl, wg_f, bg_f), gate(gbl, wg_b, bg_b),
        _heads(qg_c, GLA_HEADS) * qscale, _heads(kg_c, GLA_HEADS), _heads(vg_c, GLA_HEADS),
        gate(gfl_c, wg_f, bg_f), gate(gbl_c, wg_b, bg_b))
    yb = _gla_output(ob, rg, gla_g)
    ks_ch, vs_ch = _heads(ks_c, SWA_KV_HEADS), _heads(vs_c, SWA_KV_HEADS)
    ys = _window_attention(_apply_rope(_heads(qs, SWA_Q_HEADS), rope), _apply_rope(_heads(ks, SWA_KV_HEADS), rope),
                           _heads(vs, SWA_KV_HEADS), ks_ch, vs_ch, sink)
    y = jnp.concatenate([ya.reshape(B, L, NA_W), yb, ys.reshape(B, L, SWA_Q_W)], axis=-1)
    if not with_ctx:
        return y, None
    ya_c = _dense_attention(_heads(qa_c, NA_HEADS), ka_ch, va_ch, None)
    yb_c = _gla_output(ob_c, rg_c, gla_g)
    ys_c = _dense_attention(_heads(qs_c, SWA_Q_HEADS), ks_ch, vs_ch, sink)
    yc = jnp.concatenate([ya_c.reshape(B, M, NA_W), yb_c, ys_c.reshape(B, M, SWA_Q_W)], axis=-1)
    return y, yc


def _layer(x, xc, c_act, cc_act, w_mod, b_mod, n1, n2, w_in, rpb, wg_f, bg_f, wg_b, bg_b, gla_g, sink,
           w_out, w_ff1, w_ff2, rope, with_ctx):
    mod = (c_act @ w_mod + b_mod)[:, None, :]
    modc = (cc_act @ w_mod + b_mod)[None, None, :]
    sh1, sc1, ga1, sh2, sc2, ga2 = jnp.split(mod, 6, axis=-1)
    csh1, csc1, cga1, csh2, csc2, cga2 = jnp.split(modc, 6, axis=-1)
    h = _rmsnorm(x, n1) * (1 + sc1) + sh1
    hc = _rmsnorm(xc, n1) * (1 + csc1) + csh1
    y, yc = _mixers(h @ w_in, hc @ w_in, rpb, wg_f, bg_f, wg_b, bg_b, gla_g, sink, rope, with_ctx)
    x = x + ga1 * (y @ w_out)
    h2 = _rmsnorm(x, n2) * (1 + sc2) + sh2
    x = x + ga2 * _squared_relu_mlp(h2, w_ff1, w_ff2)
    if with_ctx:
        xc = xc + cga1 * (yc @ w_out)
        hc2 = _rmsnorm(xc, n2) * (1 + csc2) + csh2
        xc = xc + cga2 * _squared_relu_mlp(hc2, w_ff1, w_ff2)
    return x, xc


def setup_inputs(seed: int = 0) -> dict:
    key = jax.random.key(seed)
    ks = jax.random.split(key, 20)
    f32 = jnp.float32
    nrm = lambda k, shape, s: jax.random.normal(k, shape, f32) * s
    return {
        "x": nrm(ks[0], (BATCH, SEQ, D_MODEL), 1.0),
        "c": nrm(ks[1], (BATCH, D_MODEL), 1.0),
        "ctx": nrm(ks[2], (BATCH, CTX_LEN, D_MODEL), 1.0),
        "c_ctx": nrm(ks[3], (D_MODEL,), 1.0),
        "w_mod": nrm(ks[4], (DEPTH, D_MODEL, 6 * D_MODEL), 0.5 * D_MODEL ** -0.5),
        "b_mod": nrm(ks[5], (DEPTH, 6 * D_MODEL), 0.02),
        "norm1_g": 1.0 + nrm(ks[6], (DEPTH, D_MODEL), 0.02),
        "norm2_g": 1.0 + nrm(ks[7], (DEPTH, D_MODEL), 0.02),
        "w_in": nrm(ks[8], (DEPTH, D_MODEL, D_IN), D_MODEL ** -0.5),
        "na_rpb": nrm(ks[9], (DEPTH, NA_HEADS, 2 * NA_KH - 1, 2 * NA_KW - 1), 0.1),
        "gla_wg_fwd": nrm(ks[10], (DEPTH, GLA_GATE_RANK, GLA_QK_W), GLA_GATE_RANK ** -0.5),
        "gla_bg_fwd": nrm(ks[11], (DEPTH, GLA_QK_W), 0.1),
        "gla_wg_bwd": nrm(ks[12], (DEPTH, GLA_GATE_RANK, GLA_QK_W), GLA_GATE_RANK ** -0.5),
        "gla_bg_bwd": nrm(ks[13], (DEPTH, GLA_QK_W), 0.1),
        "gla_norm_g": 1.0 + nrm(ks[14], (DEPTH, GLA_DV), 0.02),
        "swa_sink": nrm(ks[15], (DEPTH, SWA_Q_HEADS), 0.5),
        "w_out": nrm(ks[16], (DEPTH, D_MIX, D_MODEL), D_MIX ** -0.5),
        "w_ff1": nrm(ks[17], (DEPTH, D_MODEL, D_FF), D_MODEL ** -0.5),
        "w_ff2": nrm(ks[18], (DEPTH, D_FF, D_MODEL), D_FF ** -0.5),
        "final_norm_g": 1.0 + nrm(ks[19], (D_MODEL,), 0.02),
    }


def reference(x, c, ctx, c_ctx, w_mod, b_mod, norm1_g, norm2_g, w_in, na_rpb, gla_wg_fwd, gla_bg_fwd,
              gla_wg_bwd, gla_bg_bwd, gla_norm_g, swa_sink, w_out, w_ff1, w_ff2, final_norm_g):
    rope = _axial_rope(x.shape[1])
    c_act = jax.nn.silu(c)
    cc_act = jax.nn.silu(c_ctx)
    xc = ctx
    for i in range(DEPTH):
        x, xc = _layer(x, xc, c_act, cc_act, w_mod[i], b_mod[i], norm1_g[i], norm2_g[i], w_in[i], na_rpb[i],
                       gla_wg_fwd[i], gla_bg_fwd[i], gla_wg_bwd[i], gla_bg_bwd[i], gla_norm_g[i], swa_sink[i],
                       w_out[i], w_ff1[i], w_ff2[i], rope, i < DEPTH - 1)
    return _rmsnorm(x, final_norm_g)
```

```python
import functools

import numpy as np
import jax
import jax.numpy as jnp
from jax import lax
from jax.experimental import pallas as pl
from jax.experimental.pallas import tpu as pltpu

F32 = jnp.float32
BF16 = jnp.bfloat16

GRID_W = 64
HEAD_DIM = 128
LANES = 128
NA_HEADS = 4
NA_KH = 8
NA_KW = 16
GLA_HEADS = 4
GLA_DK = 64
GLA_GATE_RANK = 16
GLA_GATE_NORM = 16.0
GLA_CHUNK = 64
GLA_PAIRS = GLA_HEADS // 2
SWA_Q_HEADS = 8
SWA_KV_HEADS = 2
SWA_GROUP = SWA_Q_HEADS // SWA_KV_HEADS
SWA_WINDOW = 128
SWA_BLOCK = 128
ROPE_THETA = 10000.0
EPS = 1e-6
NEG_INF = -1e30

ZB_QA, ZB_KA, ZB_VA = 0, 4, 8
ZB_QG, ZB_KG, ZB_VG, ZB_RG = 12, 14, 16, 20
ZB_QS, ZB_KS, ZB_VS = 24, 32, 34
ZB_GATE = 36
N_ZB = 37
ROPE_BLOCKS = tuple(range(ZB_QS, ZB_VS))
ATTN_SCALE = HEAD_DIM ** -0.5
Q_SCALE = {**{j: ATTN_SCALE for j in range(ZB_QA, ZB_KA)},
           **{j: GLA_DK ** -0.5 for j in range(ZB_QG, ZB_KG)},
           **{j: ATTN_SCALE for j in range(ZB_QS, ZB_KS)}}

NA_GROUP_ROWS = 4
NA_KEY_ROWS = 12
GLA_LEVELS = (32, 16, 8, 4, 2, 1)

VMEM_LIMIT = 56 * 1024 * 1024


def _cparams(sem):
    return pltpu.CompilerParams(dimension_semantics=sem, vmem_limit_bytes=VMEM_LIMIT)


def _dot(a, b):
    return jnp.dot(a, b, preferred_element_type=F32)


def _dot_nt(a, b):
    return lax.dot_general(a, b, (((1,), (1,)), ((), ())), preferred_element_type=F32)


def _dot_tn(a, b):
    return lax.dot_general(a, b, (((0,), (0,)), ((), ())), preferred_element_type=F32)


def _mod_kernel(a_ref, w_ref, b_ref, o_ref):
    a = a_ref[...]
    act = a / (1.0 + jnp.exp(-a))
    o_ref[0] = _dot(act.astype(BF16), w_ref[0].astype(BF16)) + b_ref[0]


def _modulation(cond_rows, w_mod, b_mod):
    depth, d, n = w_mod.shape
    tn = 1024
    return pl.pallas_call(
        _mod_kernel,
        grid=(depth, n // tn),
        in_specs=[pl.BlockSpec((8, d), lambda l, j: (0, 0)),
                  pl.BlockSpec((1, d, tn), lambda l, j: (l, 0, j)),
                  pl.BlockSpec((1, 1, tn), lambda l, j: (l, 0, j))],
        out_specs=pl.BlockSpec((1, 8, tn), lambda l, j: (l, 0, j)),
        out_shape=jax.ShapeDtypeStruct((depth, 8, n), F32),
        compiler_params=_cparams(("parallel", "parallel")),
        name="modulation",
    )(cond_rows, w_mod, b_mod.reshape(depth, 1, n))


def _rope_swap(a):
    lane = lax.broadcasted_iota(jnp.int32, a.shape, 1)
    first = (lane % 64) < 32
    return jnp.where(first, pltpu.roll(a, 96, 1), pltpu.roll(a, 32, 1))


def _inproj_kernel(x_ref, mod_ref, g_ref, w_ref, cos_ref, sin_ref, o_ref, *, rope, chunk):
    x = x_ref[...]
    ms = jnp.mean(x * x, axis=-1, keepdims=True)
    y = x * lax.rsqrt(ms + EPS) * g_ref[...]
    h = (y * (1.0 + mod_ref[0, 1:2, :]) + mod_ref[0, 0:1, :]).astype(BF16)
    for j0 in range(0, N_ZB, chunk):
        nblk = min(chunk, N_ZB - j0)
        acc = _dot(h, w_ref[:, j0 * LANES:(j0 + nblk) * LANES])
        for jj in range(nblk):
            j = j0 + jj
            a = acc[:, jj * LANES:(jj + 1) * LANES]
            if j in Q_SCALE:
                a = a * Q_SCALE[j]
            if rope and j in ROPE_BLOCKS:
                a = a * cos_ref[...] + _rope_swap(a) * sin_ref[...]
            o_ref[j] = a.astype(BF16)


def _inproj(xt, mod, gain, w, cos, sin, *, tm, tiles_per_batch, ctx_row):
    t, d = xt.shape
    rope = ctx_row is None
    if rope:
        mod_map = lambda i: (i // tiles_per_batch, 0, 0)
        pos_map = lambda i: (i % tiles_per_batch, 0)
    else:
        mod_map = lambda i: (ctx_row, 0, 0)
        pos_map = lambda i: (0, 0)
    return pl.pallas_call(
        functools.partial(_inproj_kernel, rope=rope, chunk=4),
        grid=(t // tm,),
        in_specs=[pl.BlockSpec((tm, d), lambda i: (i, 0)),
                  pl.BlockSpec((1, 6, d), mod_map),
                  pl.BlockSpec((1, d), lambda i: (0, 0)),
                  pl.BlockSpec((d, N_ZB * LANES), lambda i: (0, 0), pipeline_mode=pl.Buffered(1)),
                  pl.BlockSpec((tm, LANES), pos_map),
                  pl.BlockSpec((tm, LANES), pos_map)],
        out_specs=pl.BlockSpec((N_ZB, tm, LANES), lambda i: (0, i, 0)),
        out_shape=jax.ShapeDtypeStruct((N_ZB, t, LANES), BF16),
        compiler_params=_cparams(("parallel",)),
        name="in_proj",
    )(xt, mod, gain.reshape(1, d), w, cos, sin)


def _na_kernel(q_ref, k_ref, v_ref, kc_ref, vc_ref, bias_ref, o_ref, *, n_groups):
    gq = NA_GROUP_ROWS * GRID_W
    gk = NA_KEY_ROWS * GRID_W
    max_base = n_groups * NA_GROUP_ROWS - NA_KEY_ROWS
    kc = kc_ref[0]
    vc = vc_ref[0]

    def body(g, carry):
        base = jnp.clip(g * NA_GROUP_ROWS - NA_KH // 2, 0, max_base)
        kstart = pl.multiple_of(base * GRID_W, GRID_W)
        qstart = pl.multiple_of(g * gq, gq)
        q = q_ref[0, pl.ds(qstart, gq), :]
        k = k_ref[0, pl.ds(kstart, gk), :]
        v = v_ref[0, pl.ds(kstart, gk), :]
        sel = jnp.where(g == 0, 0, jnp.where(g == n_groups - 1, 2, 1))
        s_loc = _dot_nt(q, k) + bias_ref[0, sel]
        s_ctx = _dot_nt(q, kc)
        m = jnp.maximum(jnp.max(s_loc, axis=-1, keepdims=True), jnp.max(s_ctx, axis=-1, keepdims=True))
        p_loc = jnp.exp(s_loc - m)
        p_ctx = jnp.exp(s_ctx - m)
        l = jnp.sum(p_loc, axis=-1, keepdims=True) + jnp.sum(p_ctx, axis=-1, keepdims=True)
        o = _dot(p_loc.astype(BF16), v) + _dot(p_ctx.astype(BF16), vc)
        o_ref[pl.ds(qstart, gq), :] = (o / l).astype(BF16)
        return carry

    lax.fori_loop(0, n_groups, body, 0)


def _na_bias_tables(rpb, rows):
    kh = NA_KH
    tabs = []
    for r0, base in ((0, 0), (kh // 2, 0), (rows - NA_GROUP_ROWS, rows - NA_KEY_ROWS)):
        rq = np.arange(NA_GROUP_ROWS)[:, None, None, None]
        c = np.arange(GRID_W)[None, :, None, None]
        ki = np.arange(NA_KEY_ROWS)[None, None, :, None]
        kc = np.arange(GRID_W)[None, None, None, :]
        r = r0 + rq
        rs = np.clip(r - kh // 2, 0, rows - kh)
        krow = base + ki
        cs = np.clip(c - NA_KW // 2, 0, GRID_W - NA_KW)
        valid = (krow >= rs) & (krow < rs + kh) & (kc >= cs) & (kc < cs + NA_KW)
        ridx = np.clip(krow - r + NA_KH - 1, 0, 2 * NA_KH - 2)
        cidx = np.clip(kc - c + NA_KW - 1, 0, 2 * NA_KW - 2)
        shape = (NA_GROUP_ROWS, GRID_W, NA_KEY_ROWS, GRID_W)
        valid = np.broadcast_to(valid, shape).reshape(NA_GROUP_ROWS * GRID_W, NA_KEY_ROWS * GRID_W)
        ridx = np.broadcast_to(ridx, shape).reshape(valid.shape)
        cidx = np.broadcast_to(cidx, shape).reshape(valid.shape)
        vals = rpb[:, ridx, cidx].astype(F32)
        tabs.append(jnp.where(jnp.asarray(valid)[None], vals, NEG_INF))
    return jnp.stack(tabs, axis=1)


def _neighbourhood(z, zc, bias, *, batch, seq, n_ctx):
    rows = seq // GRID_W
    n_groups = rows // NA_GROUP_ROWS
    gq = NA_GROUP_ROWS * GRID_W
    gk = NA_KEY_ROWS * GRID_W
    return pl.pallas_call(
        functools.partial(_na_kernel, n_groups=n_groups),
        grid=(batch, NA_HEADS),
        in_specs=[pl.BlockSpec((1, seq, LANES), lambda b, h: (ZB_QA + h, b, 0)),
                  pl.BlockSpec((1, seq, LANES), lambda b, h: (ZB_KA + h, b, 0)),
                  pl.BlockSpec((1, seq, LANES), lambda b, h: (ZB_VA + h, b, 0)),
                  pl.BlockSpec((1, n_ctx, LANES), lambda b, h: (ZB_KA + h, b, 0)),
                  pl.BlockSpec((1, n_ctx, LANES), lambda b, h: (ZB_VA + h, b, 0)),
                  pl.BlockSpec((1, 3, gq, gk), lambda b, h: (h, 0, 0, 0))],
        out_specs=pl.BlockSpec((seq, LANES), lambda b, h: (b, h)),
        out_shape=jax.ShapeDtypeStruct((batch * seq, NA_HEADS * LANES), BF16),
        compiler_params=_cparams(("parallel", "parallel")),
        name="neighbourhood_attn",
    )(z, z, z, zc, zc, bias)


def _swa_kernel(sink_ref, q_ref, k_ref, v_ref, kc_ref, vc_ref, mask_ref, o_ref, *, blocks_per_step, n_blocks):
    grp = pl.program_id(1)
    step = pl.program_id(2)
    band = 3 * SWA_BLOCK
    kc = kc_ref[0]
    vc = vc_ref[0]

    def body(j, carry):
        n = step * blocks_per_step + j
        start = pl.multiple_of(jnp.clip((n - 1) * SWA_BLOCK, 0, (n_blocks - 3) * SWA_BLOCK), SWA_BLOCK)
        qs = pl.multiple_of(j * SWA_BLOCK, SWA_BLOCK)
        q = q_ref[:, pl.ds(qs, SWA_BLOCK), :].reshape(SWA_GROUP * SWA_BLOCK, LANES)
        k = k_ref[0, pl.ds(start, band), :]
        v = v_ref[0, pl.ds(start, band), :]
        sel = jnp.where(n == 0, 0, jnp.where(n == n_blocks - 1, 2, 1))
        msk = mask_ref[sel]
        s_loc = _dot_nt(q, k)
        s_ctx = _dot_nt(q, kc)
        for hq in range(SWA_GROUP):
            sl = s_loc[hq * SWA_BLOCK:(hq + 1) * SWA_BLOCK] + msk
            sc = s_ctx[hq * SWA_BLOCK:(hq + 1) * SWA_BLOCK]
            snk = sink_ref[grp * SWA_GROUP + hq]
            m = jnp.maximum(jnp.maximum(jnp.max(sl, axis=-1, keepdims=True),
                                        jnp.max(sc, axis=-1, keepdims=True)), snk)
            p_l = jnp.exp(sl - m)
            p_c = jnp.exp(sc - m)
            l = (jnp.sum(p_l, axis=-1, keepdims=True) + jnp.sum(p_c, axis=-1, keepdims=True)
                 + jnp.exp(snk - m))
            o = _dot(p_l.astype(BF16), v) + _dot(p_c.astype(BF16), vc)
            o_ref[pl.ds(qs, SWA_BLOCK), hq * LANES:(hq + 1) * LANES] = (o / l).astype(BF16)
        return carry

    lax.fori_loop(0, blocks_per_step, body, 0)


def _swa_masks():
    iq = np.arange(SWA_BLOCK)[:, None]
    ik = np.arange(3 * SWA_BLOCK)[None, :]
    tabs = [np.where(np.abs(ik - off - iq) <= SWA_WINDOW, 0.0, NEG_INF) for off in (0, SWA_BLOCK, 2 * SWA_BLOCK)]
    return jnp.asarray(np.stack(tabs), F32)


def _window_attention(z, zc, sink, *, batch, seq, n_ctx, tq):
    n_blocks = seq // SWA_BLOCK
    steps = seq // tq
    return pl.pallas_call(
        functools.partial(_swa_kernel, blocks_per_step=tq // SWA_BLOCK, n_blocks=n_blocks),
        grid=(batch, SWA_KV_HEADS, steps),
        in_specs=[pl.BlockSpec(memory_space=pltpu.SMEM),
                  pl.BlockSpec((SWA_GROUP, tq, LANES), lambda b, g, i: (ZB_QS // SWA_GROUP + g, b * steps + i, 0)),
                  pl.BlockSpec((1, seq, LANES), lambda b, g, i: (ZB_KS + g, b, 0)),
                  pl.BlockSpec((1, seq, LANES), lambda b, g, i: (ZB_VS + g, b, 0)),
                  pl.BlockSpec((1, n_ctx, LANES), lambda b, g, i: (ZB_KS + g, b, 0)),
                  pl.BlockSpec((1, n_ctx, LANES), lambda b, g, i: (ZB_VS + g, b, 0)),
                  pl.BlockSpec((3, SWA_BLOCK, 3 * SWA_BLOCK), lambda b, g, i: (0, 0, 0))],
        out_specs=pl.BlockSpec((tq, SWA_GROUP * LANES), lambda b, g, i: (b * steps + i, g)),
        out_shape=jax.ShapeDtypeStruct((batch * seq, SWA_Q_HEADS * LANES), BF16),
        compiler_params=_cparams(("parallel", "parallel", "parallel")),
        name="window_attn",
    )(sink, z, z, z, zc, zc, _swa_masks())


def _ctx_attn_kernel(sink_ref, q_ref, k_ref, v_ref, o_ref, *, use_sink):
    q = q_ref[0]
    s = _dot_nt(q, k_ref[0])
    m = jnp.max(s, axis=-1, keepdims=True)
    if use_sink:
        snk = sink_ref[pl.program_id(1)]
        m = jnp.maximum(m, snk)
    p = jnp.exp(s - m)
    l = jnp.sum(p, axis=-1, keepdims=True)
    if use_sink:
        l = l + jnp.exp(snk - m)
    o_ref[...] = (_dot(p.astype(BF16), v_ref[0]) / l).astype(BF16)


def _ctx_attention(zc, sink, *, batch, n_ctx, q_blk, k_blk, v_blk, heads, group, use_sink):
    return pl.pallas_call(
        functools.partial(_ctx_attn_kernel, use_sink=use_sink),
        grid=(batch, heads),
        in_specs=[pl.BlockSpec(memory_space=pltpu.SMEM),
                  pl.BlockSpec((1, n_ctx, LANES), lambda b, h: (q_blk + h, b, 0)),
                  pl.BlockSpec((1, n_ctx, LANES), lambda b, h: (k_blk + h // group, b, 0)),
                  pl.BlockSpec((1, n_ctx, LANES), lambda b, h: (v_blk + h // group, b, 0))],
        out_specs=pl.BlockSpec((n_ctx, LANES), lambda b, h: (b, h)),
        out_shape=jax.ShapeDtypeStruct((batch * n_ctx, heads * LANES), BF16),
        compiler_params=_cparams(("parallel", "parallel")),
        name="context_attn",
    )(sink, zc, zc, zc)


def _gla_matrices():
    c = GLA_CHUNK
    i = np.arange(c)[:, None]
    t = np.arange(c)[None, :]
    incl = (t <= i)
    after = (t > i)
    mq, mk, masks = [], [], [(i == t)]
    for s in GLA_LEVELS:
        blk_start = (i // s) * s
        nxt = (i // s + 1) * s
        if s > 1:
            mq.append((t > blk_start) & (t <= i))
        mk.append((t > i) & (t <= nxt) & (nxt < c))
        bi = np.arange(c)[:, None] // s
        bj = np.arange(c)[None, :] // s
        masks.append((bi % 2 == 1) & (bj == bi - 1))
    fwd_out = np.concatenate([incl] + mq + mk, axis=0).astype(np.float32)
    fwd_scan = np.concatenate([after, np.ones((8, c), bool)], axis=0).astype(np.float32)
    fwd_mask = np.stack(masks).astype(np.float32)

    def flip_rows(m):
        return m.reshape(-1, c, c)[:, ::-1, ::-1].reshape(m.shape)

    out_m = np.stack([fwd_out, flip_rows(fwd_out)])
    scan_m = np.stack([fwd_scan, np.concatenate([flip_rows(fwd_scan[:c]), fwd_scan[c:]], axis=0)])
    mask_m = np.stack([fwd_mask, fwd_mask[:, ::-1, ::-1]])
    mask_m = np.concatenate([mask_m, mask_m], axis=2)
    tile3 = lambda m: np.concatenate([m, m, m], axis=-1)
    return (jnp.asarray(tile3(out_m), BF16), jnp.asarray(tile3(scan_m), BF16), jnp.asarray(mask_m, F32))


def _split3(g):
    g1 = g.astype(BF16)
    r1 = g - g1.astype(F32)
    g2 = r1.astype(BF16)
    g3 = (r1 - g2.astype(F32)).astype(BF16)
    return jnp.concatenate([g1, g2, g3], axis=0)


def _gates_kernel(u_ref, w_ref, b_ref, o_ref):
    u = u_ref[0]
    for d in range(2):
        for p in range(GLA_PAIRS):
            pre = _dot(u, w_ref[d, p]) + b_ref[d, p]
            logsig = jnp.minimum(pre, 0.0) - jnp.log(1.0 + jnp.exp(-jnp.abs(pre)))
            o_ref[d, p] = logsig * (1.0 / GLA_GATE_NORM)


def _gla_gates(z, wg, bg, *, tm):
    t = z.shape[1]
    return pl.pallas_call(
        _gates_kernel,
        grid=(t // tm,),
        in_specs=[pl.BlockSpec((1, tm, LANES), lambda i: (ZB_GATE, i, 0)),
                  pl.BlockSpec((2, GLA_PAIRS, LANES, LANES), lambda i: (0, 0, 0, 0)),
                  pl.BlockSpec((2, GLA_PAIRS, 1, LANES), lambda i: (0, 0, 0, 0))],
        out_specs=pl.BlockSpec((2, GLA_PAIRS, tm, LANES), lambda i: (0, 0, i, 0)),
        out_shape=jax.ShapeDtypeStruct((2, GLA_PAIRS, t, LANES), F32),
        compiler_params=_cparams(("parallel",)),
        name="gla_gates",
    )(z, wg, bg)


def _pair_lane_mask(shape):
    return lax.broadcasted_iota(jnp.int32, shape, 1) < GLA_DK


def _gla_scan_kernel(k_ref, v_ref, g_ref, m_ref, s0_ref, sprev_ref, sfin_ref, s_acc, *, chunks):
    d = pl.program_id(2)
    c = GLA_CHUNK

    @pl.when(pl.program_id(3) == 0)
    def _():
        s_acc[...] = s0_ref[0, 0, 0]

    low = _pair_lane_mask((LANES, LANES))

    def body(jj, carry):
        cc = jj + d * (chunks - 1 - 2 * jj)
        rows = pl.ds(pl.multiple_of(cc * c, c), c)
        e = _dot(m_ref[0], _split3(g_ref[0, 0, rows, :]))
        khat = (k_ref[0, rows, :].astype(F32) * jnp.exp(e[0:c])).astype(BF16)
        vv = jnp.concatenate([v_ref[0, rows, :], v_ref[1, rows, :]], axis=1)
        full = _dot_tn(vv, khat)
        upd = jnp.where(low, full[0:LANES], full[LANES:2 * LANES])
        s_old = s_acc[...]
        sprev_ref[0, 0, 0, cc] = s_old.astype(BF16)
        s_acc[...] = jnp.exp(e[c:c + 1]) * s_old + upd
        return carry

    lax.fori_loop(0, chunks, body, 0)
    sfin_ref[0, 0, 0] = s_acc[...]


def _gla_scan(z, gates, scan_m, s0, *, batch, seq, tq):
    nt = seq // tq
    chunks = tq // GLA_CHUNK
    nc = seq // GLA_CHUNK
    tile = lambda d, i: i + d * (nt - 1 - 2 * i)
    return pl.pallas_call(
        functools.partial(_gla_scan_kernel, chunks=chunks),
        grid=(batch, GLA_PAIRS, 2, nt),
        in_specs=[pl.BlockSpec((1, tq, LANES), lambda b, p, d, i: (ZB_KG + p, b * nt + tile(d, i), 0)),
                  pl.BlockSpec((2, tq, LANES), lambda b, p, d, i: (ZB_VG // 2 + p, b * nt + tile(d, i), 0)),
                  pl.BlockSpec((1, 1, tq, LANES), lambda b, p, d, i: (d, p, b * nt + tile(d, i), 0)),
                  pl.BlockSpec((1,) + scan_m.shape[1:], lambda b, p, d, i: (d, 0, 0)),
                  pl.BlockSpec((1, 1, 1, LANES, LANES), lambda b, p, d, i: (b, p, d, 0, 0))],
        out_specs=[pl.BlockSpec((1, 1, 1, chunks, LANES, LANES), lambda b, p, d, i: (b, p, d, tile(d, i), 0, 0)),
                   pl.BlockSpec((1, 1, 1, LANES, LANES), lambda b, p, d, i: (b, p, d, 0, 0))],
        out_shape=[jax.ShapeDtypeStruct((batch, GLA_PAIRS, 2, nc, LANES, LANES), BF16),
                   jax.ShapeDtypeStruct((batch, GLA_PAIRS, 2, LANES, LANES), F32)],
        scratch_shapes=[pltpu.VMEM((LANES, LANES), F32)],
        compiler_params=_cparams(("parallel", "parallel", "parallel", "arbitrary")),
        name="gla_scan",
    )(z, z, gates, scan_m, s0)


def _gla_out_kernel(q_ref, k_ref, v_ref, r_ref, g_ref, sprev_ref, m_ref, mask_ref, gain_ref, o_ref, *, chunks):
    c = GLA_CHUNK
    low = _pair_lane_mask((c, LANES))

    def stack_heads(a):
        return jnp.concatenate([jnp.where(low, a, 0.0), jnp.where(low, 0.0, a)], axis=0).astype(BF16)

    def body(cc, carry):
        rows = pl.ds(pl.multiple_of(cc * c, c), c)
        kb = k_ref[0, rows, :]
        q = q_ref[0, rows, :].astype(F32)
        k = kb.astype(F32)
        vv = jnp.concatenate([v_ref[0, rows, :], v_ref[1, rows, :]], axis=1)
        o0 = jnp.zeros((c, LANES), F32)
        o1 = jnp.zeros((c, LANES), F32)
        for d in range(2):
            e = _dot(m_ref[d], _split3(g_ref[d, 0, rows, :]))
            inter = _dot_nt(stack_heads(q * jnp.exp(e[0:c])), sprev_ref[0, 0, d, cc])
            a = mask_ref[d, 0] * _dot_nt(stack_heads(q), kb)
            n_q = len(GLA_LEVELS) - 1
            for idx in range(len(GLA_LEVELS)):
                ek = e[(1 + n_q + idx) * c:(2 + n_q + idx) * c]
                qt = q * jnp.exp(e[(1 + idx) * c:(2 + idx) * c]) if idx < n_q else q
                kt = (k * jnp.exp(ek)).astype(BF16)
                a = a + mask_ref[d, 1 + idx] * _dot_nt(stack_heads(qt), kt)
            ov = _dot(a.astype(BF16), vv)
            o0 = o0 + ov[0:c, 0:LANES] + inter[0:c]
            o1 = o1 + ov[c:2 * c, LANES:2 * LANES] + inter[c:2 * c]
        for h, o in enumerate((o0, o1)):
            of = o * lax.rsqrt(jnp.mean(o * o, axis=-1, keepdims=True) + EPS) * gain_ref[...]
            r = r_ref[h, rows, :].astype(F32)
            o_ref[rows, h * LANES:(h + 1) * LANES] = (of * (r / (1.0 + jnp.exp(-r)))).astype(BF16)
        return carry

    lax.fori_loop(0, chunks, body, 0)


def _gla_output(z, gates, sprev, out_m, mask_m, gain, *, batch, seq, tq):
    nt = seq // tq
    chunks = tq // GLA_CHUNK
    return pl.pallas_call(
        functools.partial(_gla_out_kernel, chunks=chunks),
        grid=(batch, GLA_PAIRS, nt),
        in_specs=[pl.BlockSpec((1, tq, LANES), lambda b, p, i: (ZB_QG + p, b * nt + i, 0)),
                  pl.BlockSpec((1, tq, LANES), lambda b, p, i: (ZB_KG + p, b * nt + i, 0)),
                  pl.BlockSpec((2, tq, LANES), lambda b, p, i: (ZB_VG // 2 + p, b * nt + i, 0)),
                  pl.BlockSpec((2, tq, LANES), lambda b, p, i: (ZB_RG // 2 + p, b * nt + i, 0)),
                  pl.BlockSpec((2, 1, tq, LANES), lambda b, p, i: (0, p, b * nt + i, 0)),
                  pl.BlockSpec((1, 1, 2, chunks, LANES, LANES), lambda b, p, i: (b, p, 0, i, 0, 0)),
                  pl.BlockSpec(out_m.shape, lambda b, p, i: (0, 0, 0)),
                  pl.BlockSpec(mask_m.shape, lambda b, p, i: (0, 0, 0, 0)),
                  pl.BlockSpec((1, LANES), lambda b, p, i: (0, 0))],
        out_specs=pl.BlockSpec((tq, 2 * LANES), lambda b, p, i: (b * nt + i, p)),
        out_shape=jax.ShapeDtypeStruct((batch * seq, GLA_HEADS * LANES), BF16),
        compiler_params=_cparams(("parallel", "parallel", "parallel")),
        name="gla_output",
    )(z, z, z, z, gates, sprev, out_m, mask_m, gain.reshape(1, LANES))


def _outproj_kernel(ya_ref, yb_ref, ys_ref, w_ref, x_ref, mod_ref, g_ref, x1_ref, h2_ref):
    na = ya_ref.shape[1]
    nb = yb_ref.shape[1]
    proj = (_dot(ya_ref[...], w_ref[0:na, :]) + _dot(yb_ref[...], w_ref[na:na + nb, :])
            + _dot(ys_ref[...], w_ref[na + nb:, :]))
    x1 = x_ref[...] + mod_ref[0, 2:3, :] * proj
    x1_ref[...] = x1
    ms = jnp.mean(x1 * x1, axis=-1, keepdims=True)
    y = x1 * lax.rsqrt(ms + EPS) * g_ref[...]
    h2_ref[...] = (y * (1.0 + mod_ref[0, 4:5, :]) + mod_ref[0, 3:4, :]).astype(BF16)


def _outproj(ya, yb, ys, w_out, xt, mod, gain, *, tm, tiles_per_batch, ctx_row):
    t, d = xt.shape
    mod_map = (lambda i: (i // tiles_per_batch, 0, 0)) if ctx_row is None else (lambda i: (ctx_row, 0, 0))
    row = lambda i: (i, 0)
    return pl.pallas_call(
        _outproj_kernel,
        grid=(t // tm,),
        in_specs=[pl.BlockSpec((tm, ya.shape[1]), row),
                  pl.BlockSpec((tm, yb.shape[1]), row),
                  pl.BlockSpec((tm, ys.shape[1]), row),
                  pl.BlockSpec(w_out.shape, lambda i: (0, 0), pipeline_mode=pl.Buffered(1)),
                  pl.BlockSpec((tm, d), row),
                  pl.BlockSpec((1, 6, d), mod_map),
                  pl.BlockSpec((1, d), lambda i: (0, 0))],
        out_specs=[pl.BlockSpec((tm, d), row), pl.BlockSpec((tm, d), row)],
        out_shape=[jax.ShapeDtypeStruct((t, d), F32), jax.ShapeDtypeStruct((t, d), BF16)],
        compiler_params=_cparams(("parallel",)),
        name="out_proj",
    )(ya, yb, ys, w_out, xt, mod, gain.reshape(1, d))


def _mlp_kernel(h_ref, w1_ref, w2_ref, x_ref, mod_ref, g_ref, o_ref, acc_ref, *, final_norm):
    f = pl.program_id(1)

    @pl.when(f == 0)
    def _():
        acc_ref[...] = jnp.zeros_like(acc_ref)

    a = jnp.maximum(_dot(h_ref[...], w1_ref[...]), 0.0)
    acc_ref[...] += _dot((a * a).astype(BF16), w2_ref[...])

    @pl.when(f == pl.num_programs(1) - 1)
    def _():
        out = x_ref[...] + mod_ref[0, 5:6, :] * acc_ref[...]
        if final_norm:
            ms = jnp.mean(out * out, axis=-1, keepdims=True)
            out = out * lax.rsqrt(ms + EPS) * g_ref[...]
        o_ref[...] = out


def _mlp(h2, w1, w2, x1, mod, final_gain, *, tm, tf, tiles_per_batch, ctx_row, final_norm):
    t, d = x1.shape
    ff = w1.shape[1]
    mod_map = (lambda i, f: (i // tiles_per_batch, 0, 0)) if ctx_row is None else (lambda i, f: (ctx_row, 0, 0))
    return pl.pallas_call(
        functools.partial(_mlp_kernel, final_norm=final_norm),
        grid=(t // tm, ff // tf),
        in_specs=[pl.BlockSpec((tm, d), lambda i, f: (i, 0)),
                  pl.BlockSpec((d, tf), lambda i, f: (0, f)),
                  pl.BlockSpec((tf, d), lambda i, f: (f, 0)),
                  pl.BlockSpec((tm, d), lambda i, f: (i, 0)),
                  pl.BlockSpec((1, 6, d), mod_map),
                  pl.BlockSpec((1, d), lambda i, f: (0, 0))],
        out_specs=pl.BlockSpec((tm, d), lambda i, f: (i, 0)),
        out_shape=jax.ShapeDtypeStruct((t, d), F32),
        scratch_shapes=[pltpu.VMEM((tm, d), F32)],
        compiler_params=_cparams(("parallel", "arbitrary")),
        name="mlp",
    )(h2, w1, w2, x1, mod, final_gain.reshape(1, d))


def _rope_tables(seq):
    t = jnp.arange(seq, dtype=jnp.int32)
    row = (t // GRID_W).astype(F32)
    col = (t % GRID_W).astype(F32)
    n_freq = HEAD_DIM // 4
    inv = ROPE_THETA ** (-jnp.arange(n_freq, dtype=F32) / n_freq)
    ar = row[:, None] * inv[None]
    ac = col[:, None] * inv[None]
    cos = jnp.concatenate([jnp.cos(ar), jnp.cos(ar), jnp.cos(ac), jnp.cos(ac)], axis=-1)
    sin = jnp.concatenate([-jnp.sin(ar), jnp.sin(ar), -jnp.sin(ac), jnp.sin(ac)], axis=-1)
    return cos, sin


def _reorder_w_in(w_in):
    d = w_in.shape[0]
    g0 = (ZB_QS - 0) * LANES
    g1 = g0 + 2 * GLA_GATE_RANK
    pad = jnp.zeros((d, LANES - 2 * GLA_GATE_RANK), w_in.dtype)
    return jnp.concatenate([w_in[:, :g0], w_in[:, g1:], w_in[:, g0:g1], pad], axis=1).astype(BF16)


def _gate_weights(wg_f, bg_f, wg_b, bg_b):
    r = GLA_GATE_RANK
    w = jnp.zeros((2, GLA_PAIRS, LANES, LANES), F32)
    for p in range(GLA_PAIRS):
        w = w.at[0, p, 0:r, :].set(wg_f[:, p * LANES:(p + 1) * LANES])
        w = w.at[1, p, r:2 * r, :].set(wg_b[:, p * LANES:(p + 1) * LANES])
    b = jnp.stack([bg_f.reshape(GLA_PAIRS, 1, LANES), bg_b.reshape(GLA_PAIRS, 1, LANES)])
    return w.astype(BF16), b.astype(F32)


@jax.jit
def _forward(x, c, ctx, c_ctx, w_mod, b_mod, norm1_g, norm2_g, w_in, na_rpb, gla_wg_fwd, gla_bg_fwd,
             gla_wg_bwd, gla_bg_bwd, gla_norm_g, swa_sink, w_out, w_ff1, w_ff2, final_norm_g):
    batch, seq, d = x.shape
    n_ctx = ctx.shape[1]
    depth = w_mod.shape[0]
    ctx_row = batch
    tm = 512
    tpb = seq // tm

    cond = jnp.concatenate([c, c_ctx[None], jnp.zeros((8 - batch - 1, d), F32)], axis=0)
    mods = _modulation(cond, w_mod, b_mod).reshape(depth, 8, 6, d)
    cos, sin = _rope_tables(seq)
    out_m, scan_m, mask_m = _gla_matrices()
    zero_state = jnp.zeros((batch, GLA_PAIRS, 2, LANES, LANES), F32)

    xl = x.reshape(batch * seq, d)
    xc = ctx.reshape(batch * n_ctx, d)
    for i in range(depth):
        with_ctx = i < depth - 1
        mod = mods[i]
        w_in_r = _reorder_w_in(w_in[i])
        w_out_b = w_out[i].astype(BF16)
        w1 = w_ff1[i].astype(BF16)
        w2 = w_ff2[i].astype(BF16)
        wg, bg = _gate_weights(gla_wg_fwd[i], gla_bg_fwd[i], gla_wg_bwd[i], gla_bg_bwd[i])
        bias = _na_bias_tables(na_rpb[i], seq // GRID_W)

        z = _inproj(xl, mod, norm1_g[i], w_in_r, cos, sin, tm=tm, tiles_per_batch=tpb, ctx_row=None)
        zc = _inproj(xc, mod, norm1_g[i], w_in_r, cos, sin, tm=n_ctx, tiles_per_batch=1, ctx_row=ctx_row)

        gates_c = _gla_gates(zc, wg, bg, tm=n_ctx)
        gates_l = _gla_gates(z, wg, bg, tm=min(2048, seq))
        sprev_c, sfin_c = _gla_scan(zc, gates_c, scan_m, zero_state, batch=batch, seq=n_ctx, tq=n_ctx)
        sprev_l, _ = _gla_scan(z, gates_l, scan_m, sfin_c, batch=batch, seq=seq, tq=min(2048, seq))

        ya = _neighbourhood(z, zc, bias, batch=batch, seq=seq, n_ctx=n_ctx)
        yb = _gla_output(z, gates_l, sprev_l, out_m, mask_m, gla_norm_g[i], batch=batch, seq=seq, tq=512)
        ys = _window_attention(z, zc, swa_sink[i], batch=batch, seq=seq, n_ctx=n_ctx, tq=min(1024, seq))

        x1, h2 = _outproj(ya, yb, ys, w_out_b, xl, mod, norm2_g[i], tm=tm, tiles_per_batch=tpb, ctx_row=None)
        xl = _mlp(h2, w1, w2, x1, mod, final_norm_g, tm=tm, tf=1024, tiles_per_batch=tpb, ctx_row=None,
                  final_norm=not with_ctx)

        if with_ctx:
            ya_c = _ctx_attention(zc, swa_sink[i], batch=batch, n_ctx=n_ctx, q_blk=ZB_QA, k_blk=ZB_KA,
                                  v_blk=ZB_VA, heads=NA_HEADS, group=1, use_sink=False)
            yb_c = _gla_output(zc, gates_c, sprev_c, out_m, mask_m, gla_norm_g[i], batch=batch, seq=n_ctx,
                               tq=n_ctx)
            ys_c = _ctx_attention(zc, swa_sink[i], batch=batch, n_ctx=n_ctx, q_blk=ZB_QS, k_blk=ZB_KS,
                                  v_blk=ZB_VS, heads=SWA_Q_HEADS, group=SWA_GROUP, use_sink=True)
            x1c, h2c = _outproj(ya_c, yb_c, ys_c, w_out_b, xc, mod, norm2_g[i], tm=n_ctx, tiles_per_batch=1,
                                ctx_row=ctx_row)
            xc = _mlp(h2c, w1, w2, x1c, mod, final_norm_g, tm=n_ctx, tf=1024, tiles_per_batch=1,
                      ctx_row=ctx_row, final_norm=False)
    return xl.reshape(batch, seq, d)


def kernel(x, c, ctx, c_ctx, w_mod, b_mod, norm1_g, norm2_g, w_in, na_rpb, gla_wg_fwd, gla_bg_fwd, gla_wg_bwd,
           gla_bg_bwd, gla_norm_g, swa_sink, w_out, w_ff1, w_ff2, final_norm_g):
    return _forward(x, c, ctx, c_ctx, w_mod, b_mod, norm1_g, norm2_g, w_in, na_rpb, gla_wg_fwd, gla_bg_fwd,
                    gla_wg_bwd, gla_bg_bwd, gla_norm_g, swa_sink, w_out, w_ff1, w_ff2, final_norm_g)
```

```python
import functools

import numpy as np
import jax
import jax.numpy as jnp
from jax import lax
from jax.experimental import pallas as pl
from jax.experimental.pallas import tpu as pltpu

F32 = jnp.float32
BF16 = jnp.bfloat16

GRID_W = 64
HEAD_DIM = 128
LANES = 128
NA_HEADS = 4
NA_KH = 8
NA_KW = 16
GLA_HEADS = 4
GLA_DK = 64
GLA_GATE_RANK = 16
GLA_GATE_NORM = 16.0
GLA_CHUNK = 64
GLA_PAIRS = GLA_HEADS // 2
SWA_Q_HEADS = 8
SWA_KV_HEADS = 2
SWA_GROUP = SWA_Q_HEADS // SWA_KV_HEADS
SWA_WINDOW = 128
SWA_BLOCK = 128
ROPE_THETA = 10000.0
EPS = 1e-6
NEG_INF = -1e30

ZB_QA, ZB_KA, ZB_VA = 0, 4, 8
ZB_QG, ZB_KG, ZB_VG, ZB_RG = 12, 14, 16, 20
ZB_QS, ZB_KS, ZB_VS = 24, 32, 34
ZB_GATE = 36
N_ZB = 37
ROPE_BLOCKS = tuple(range(ZB_QS, ZB_VS))
ATTN_SCALE = HEAD_DIM ** -0.5
Q_SCALE = {**{j: ATTN_SCALE for j in range(ZB_QA, ZB_KA)},
           **{j: GLA_DK ** -0.5 for j in range(ZB_QG, ZB_KG)},
           **{j: ATTN_SCALE for j in range(ZB_QS, ZB_KS)}}

NA_GROUP_ROWS = 4
NA_KEY_ROWS = 12
GLA_LEVELS = (32, 16, 8, 4, 2, 1)

VMEM_LIMIT = 56 * 1024 * 1024


def _cparams(sem):
    return pltpu.CompilerParams(dimension_semantics=sem, vmem_limit_bytes=VMEM_LIMIT)


def _dot(a, b):
    return jnp.dot(a, b, preferred_element_type=F32)


def _dot_nt(a, b):
    return lax.dot_general(a, b, (((1,), (1,)), ((), ())), preferred_element_type=F32)


def _dot_tn(a, b):
    return lax.dot_general(a, b, (((0,), (0,)), ((), ())), preferred_element_type=F32)


def _mod_kernel(a_ref, w_ref, b_ref, o_ref):
    a = a_ref[...]
    act = a / (1.0 + jnp.exp(-a))
    o_ref[0] = _dot(act.astype(BF16), w_ref[0].astype(BF16)) + b_ref[0]


def _modulation(cond_rows, w_mod, b_mod):
    depth, d, n = w_mod.shape
    tn = 1024
    return pl.pallas_call(
        _mod_kernel,
        grid=(depth, n // tn),
        in_specs=[pl.BlockSpec((8, d), lambda l, j: (0, 0)),
                  pl.BlockSpec((1, d, tn), lambda l, j: (l, 0, j)),
                  pl.BlockSpec((1, 1, tn), lambda l, j: (l, 0, j))],
        out_specs=pl.BlockSpec((1, 8, tn), lambda l, j: (l, 0, j)),
        out_shape=jax.ShapeDtypeStruct((depth, 8, n), F32),
        compiler_params=_cparams(("parallel", "parallel")),
        name="modulation",
    )(cond_rows, w_mod, b_mod.reshape(depth, 1, n))


def _rope_swap(a):
    lane = lax.broadcasted_iota(jnp.int32, a.shape, 1)
    first = (lane % 64) < 32
    return jnp.where(first, pltpu.roll(a, 96, 1), pltpu.roll(a, 32, 1))


def _inproj_kernel(x_ref, mod_ref, g_ref, w_ref, cos_ref, sin_ref, o_ref, *, rope, chunk):
    x = x_ref[...]
    ms = jnp.mean(x * x, axis=-1, keepdims=True)
    y = x * lax.rsqrt(ms + EPS) * g_ref[...]
    h = (y * (1.0 + mod_ref[0, 1:2, :]) + mod_ref[0, 0:1, :]).astype(BF16)
    for j0 in range(0, N_ZB, chunk):
        nblk = min(chunk, N_ZB - j0)
        acc = _dot(h, w_ref[:, j0 * LANES:(j0 + nblk) * LANES])
        for jj in range(nblk):
            j = j0 + jj
            a = acc[:, jj * LANES:(jj + 1) * LANES]
            if j in Q_SCALE:
                a = a * Q_SCALE[j]
            if rope and j in ROPE_BLOCKS:
                a = a * cos_ref[...] + _rope_swap(a) * sin_ref[...]
            o_ref[j] = a.astype(BF16)


def _inproj(xt, mod, gain, w, cos, sin, *, tm, tiles_per_batch, ctx_row):
    t, d = xt.shape
    rope = ctx_row is None
    if rope:
        mod_map = lambda i: (i // tiles_per_batch, 0, 0)
        pos_map = lambda i: (i % tiles_per_batch, 0)
    else:
        mod_map = lambda i: (ctx_row, 0, 0)
        pos_map = lambda i: (0, 0)
    return pl.pallas_call(
        functools.partial(_inproj_kernel, rope=rope, chunk=4),
        grid=(t // tm,),
        in_specs=[pl.BlockSpec((tm, d), lambda i: (i, 0)),
                  pl.BlockSpec((1, 6, d), mod_map),
                  pl.BlockSpec((1, d), lambda i: (0, 0)),
                  pl.BlockSpec((d, N_ZB * LANES), lambda i: (0, 0), pipeline_mode=pl.Buffered(1)),
                  pl.BlockSpec((tm, LANES), pos_map),
                  pl.BlockSpec((tm, LANES), pos_map)],
        out_specs=pl.BlockSpec((N_ZB, tm, LANES), lambda i: (0, i, 0)),
        out_shape=jax.ShapeDtypeStruct((N_ZB, t, LANES), BF16),
        compiler_params=_cparams(("parallel",)),
        name="in_proj",
    )(xt, mod, gain.reshape(1, d), w, cos, sin)


def _na_kernel(q_ref, k_ref, v_ref, kc_ref, vc_ref, bias_ref, o_ref, *, n_groups):
    gq = NA_GROUP_ROWS * GRID_W
    gk = NA_KEY_ROWS * GRID_W
    max_base = n_groups * NA_GROUP_ROWS - NA_KEY_ROWS
    kc = kc_ref[0]
    vc = vc_ref[0]

    def body(g, carry):
        base = jnp.clip(g * NA_GROUP_ROWS - NA_KH // 2, 0, max_base)
        kstart = pl.multiple_of(base * GRID_W, GRID_W)
        qstart = pl.multiple_of(g * gq, gq)
        q = q_ref[0, pl.ds(qstart, gq), :]
        k = k_ref[0, pl.ds(kstart, gk), :]
        v = v_ref[0, pl.ds(kstart, gk), :]
        sel = jnp.where(g == 0, 0, jnp.where(g == n_groups - 1, 2, 1))
        s_loc = _dot_nt(q, k) + bias_ref[0, sel]
        s_ctx = _dot_nt(q, kc)
        m = jnp.maximum(jnp.max(s_loc, axis=-1, keepdims=True), jnp.max(s_ctx, axis=-1, keepdims=True))
        p_loc = jnp.exp(s_loc - m)
        p_ctx = jnp.exp(s_ctx - m)
        l = jnp.sum(p_loc, axis=-1, keepdims=True) + jnp.sum(p_ctx, axis=-1, keepdims=True)
        o = _dot(p_loc.astype(BF16), v) + _dot(p_ctx.astype(BF16), vc)
        o_ref[pl.ds(qstart, gq), :] = (o / l).astype(BF16)
        return carry

    lax.fori_loop(0, n_groups, body, 0, unroll=2)


def _na_bias_tables(rpb, rows):
    kh = NA_KH
    heads = rpb.shape[0]
    c = np.arange(GRID_W)[:, None]
    kc = np.arange(GRID_W)[None, :]
    cs = np.clip(c - NA_KW // 2, 0, GRID_W - NA_KW)
    col_ok = (kc >= cs) & (kc < cs + NA_KW)
    d = np.arange(2 * NA_KW - 1)[:, None, None]
    onehot = ((kc - c + NA_KW - 1)[None] == d) & col_ok[None]
    toep = jnp.einsum('hrd,dck->hrck', rpb.astype(F32), jnp.asarray(onehot, F32),
                      precision=lax.Precision.HIGHEST)
    toep = toep + jnp.asarray(np.where(col_ok, 0.0, NEG_INF), F32)
    dead = jnp.full((heads, GRID_W, GRID_W), NEG_INF, F32)
    tabs = []
    for r0, base in ((0, 0), (kh // 2, 0), (rows - NA_GROUP_ROWS, rows - NA_KEY_ROWS)):
        blk_rows = []
        for rq in range(NA_GROUP_ROWS):
            r = r0 + rq
            rs = min(max(r - kh // 2, 0), rows - kh)
            blks = []
            for ki in range(NA_KEY_ROWS):
                krow = base + ki
                blks.append(toep[:, krow - r + NA_KH - 1] if rs <= krow < rs + kh else dead)
            blk_rows.append(jnp.concatenate(blks, axis=-1))
        tabs.append(jnp.concatenate(blk_rows, axis=-2))
    return jnp.stack(tabs, axis=1)


def _neighbourhood(z, zc, bias, *, batch, seq, n_ctx):
    rows = seq // GRID_W
    n_groups = rows // NA_GROUP_ROWS
    gq = NA_GROUP_ROWS * GRID_W
    gk = NA_KEY_ROWS * GRID_W
    return pl.pallas_call(
        functools.partial(_na_kernel, n_groups=n_groups),
        grid=(batch, NA_HEADS),
        in_specs=[pl.BlockSpec((1, seq, LANES), lambda b, h: (ZB_QA + h, b, 0)),
                  pl.BlockSpec((1, seq, LANES), lambda b, h: (ZB_KA + h, b, 0)),
                  pl.BlockSpec((1, seq, LANES), lambda b, h: (ZB_VA + h, b, 0)),
                  pl.BlockSpec((1, n_ctx, LANES), lambda b, h: (ZB_KA + h, b, 0)),
                  pl.BlockSpec((1, n_ctx, LANES), lambda b, h: (ZB_VA + h, b, 0)),
                  pl.BlockSpec((1, 3, gq, gk), lambda b, h: (h, 0, 0, 0))],
        out_specs=pl.BlockSpec((seq, LANES), lambda b, h: (b, h)),
        out_shape=jax.ShapeDtypeStruct((batch * seq, NA_HEADS * LANES), BF16),
        compiler_params=_cparams(("parallel", "parallel")),
        name="neighbourhood_attn",
    )(z, z, z, zc, zc, bias)


def _swa_kernel(sink_ref, q_ref, k_ref, v_ref, kc_ref, vc_ref, mask_ref, o_ref, *, blocks_per_step, n_blocks):
    grp = pl.program_id(1)
    step = pl.program_id(2)
    band = 3 * SWA_BLOCK
    kc = kc_ref[0]
    vc = vc_ref[0]

    def body(j, carry):
        n = step * blocks_per_step + j
        start = pl.multiple_of(jnp.clip((n - 1) * SWA_BLOCK, 0, (n_blocks - 3) * SWA_BLOCK), SWA_BLOCK)
        qs = pl.multiple_of(j * SWA_BLOCK, SWA_BLOCK)
        q = q_ref[:, pl.ds(qs, SWA_BLOCK), :].reshape(SWA_GROUP * SWA_BLOCK, LANES)
        k = k_ref[0, pl.ds(start, band), :]
        v = v_ref[0, pl.ds(start, band), :]
        sel = jnp.where(n == 0, 0, jnp.where(n == n_blocks - 1, 2, 1))
        msk = mask_ref[sel]
        s_loc = _dot_nt(q, k)
        s_ctx = _dot_nt(q, kc)
        for hq in range(SWA_GROUP):
            sl = s_loc[hq * SWA_BLOCK:(hq + 1) * SWA_BLOCK] + msk
            sc = s_ctx[hq * SWA_BLOCK:(hq + 1) * SWA_BLOCK]
            snk = sink_ref[grp * SWA_GROUP + hq]
            m = jnp.maximum(jnp.maximum(jnp.max(sl, axis=-1, keepdims=True),
                                        jnp.max(sc, axis=-1, keepdims=True)), snk)
            p_l = jnp.exp(sl - m)
            p_c = jnp.exp(sc - m)
            l = (jnp.sum(p_l, axis=-1, keepdims=True) + jnp.sum(p_c, axis=-1, keepdims=True)
                 + jnp.exp(snk - m))
            o = _dot(p_l.astype(BF16), v) + _dot(p_c.astype(BF16), vc)
            o_ref[pl.ds(qs, SWA_BLOCK), hq * LANES:(hq + 1) * LANES] = (o / l).astype(BF16)
        return carry

    lax.fori_loop(0, blocks_per_step, body, 0, unroll=2)


def _swa_masks():
    iq = np.arange(SWA_BLOCK)[:, None]
    ik = np.arange(3 * SWA_BLOCK)[None, :]
    tabs = [np.where(np.abs(ik - off - iq) <= SWA_WINDOW, 0.0, NEG_INF) for off in (0, SWA_BLOCK, 2 * SWA_BLOCK)]
    return jnp.asarray(np.stack(tabs), F32)


def _window_attention(z, zc, sink, *, batch, seq, n_ctx, tq):
    n_blocks = seq // SWA_BLOCK
    steps = seq // tq
    return pl.pallas_call(
        functools.partial(_swa_kernel, blocks_per_step=tq // SWA_BLOCK, n_blocks=n_blocks),
        grid=(batch, SWA_KV_HEADS, steps),
        in_specs=[pl.BlockSpec(memory_space=pltpu.SMEM),
                  pl.BlockSpec((SWA_GROUP, tq, LANES), lambda b, g, i: (ZB_QS // SWA_GROUP + g, b * steps + i, 0)),
                  pl.BlockSpec((1, seq, LANES), lambda b, g, i: (ZB_KS + g, b, 0)),
                  pl.BlockSpec((1, seq, LANES), lambda b, g, i: (ZB_VS + g, b, 0)),
                  pl.BlockSpec((1, n_ctx, LANES), lambda b, g, i: (ZB_KS + g, b, 0)),
                  pl.BlockSpec((1, n_ctx, LANES), lambda b, g, i: (ZB_VS + g, b, 0)),
                  pl.BlockSpec((3, SWA_BLOCK, 3 * SWA_BLOCK), lambda b, g, i: (0, 0, 0))],
        out_specs=pl.BlockSpec((tq, SWA_GROUP * LANES), lambda b, g, i: (b * steps + i, g)),
        out_shape=jax.ShapeDtypeStruct((batch * seq, SWA_Q_HEADS * LANES), BF16),
        compiler_params=_cparams(("parallel", "parallel", "parallel")),
        name="window_attn",
    )(sink, z, z, z, zc, zc, _swa_masks())


def _ctx_attn_kernel(sink_ref, q_ref, k_ref, v_ref, o_ref, *, use_sink):
    q = q_ref[0]
    s = _dot_nt(q, k_ref[0])
    m = jnp.max(s, axis=-1, keepdims=True)
    if use_sink:
        snk = sink_ref[pl.program_id(1)]
        m = jnp.maximum(m, snk)
    p = jnp.exp(s - m)
    l = jnp.sum(p, axis=-1, keepdims=True)
    if use_sink:
        l = l + jnp.exp(snk - m)
    o_ref[...] = (_dot(p.astype(BF16), v_ref[0]) / l).astype(BF16)


def _ctx_attention(zc, sink, *, batch, n_ctx, q_blk, k_blk, v_blk, heads, group, use_sink):
    return pl.pallas_call(
        functools.partial(_ctx_attn_kernel, use_sink=use_sink),
        grid=(batch, heads),
        in_specs=[pl.BlockSpec(memory_space=pltpu.SMEM),
                  pl.BlockSpec((1, n_ctx, LANES), lambda b, h: (q_blk + h, b, 0)),
                  pl.BlockSpec((1, n_ctx, LANES), lambda b, h: (k_blk + h // group, b, 0)),
                  pl.BlockSpec((1, n_ctx, LANES), lambda b, h: (v_blk + h // group, b, 0))],
        out_specs=pl.BlockSpec((n_ctx, LANES), lambda b, h: (b, h)),
        out_shape=jax.ShapeDtypeStruct((batch * n_ctx, heads * LANES), BF16),
        compiler_params=_cparams(("parallel", "parallel")),
        name="context_attn",
    )(sink, zc, zc, zc)


def _gla_matrices():
    c = GLA_CHUNK
    i = np.arange(c)[:, None]
    t = np.arange(c)[None, :]
    incl = (t <= i)
    after = (t > i)
    mq, mk, masks = [], [], [(i == t)]
    for s in GLA_LEVELS:
        blk_start = (i // s) * s
        nxt = (i // s + 1) * s
        if s > 1:
            mq.append((t > blk_start) & (t <= i))
        mk.append((t > i) & (t <= nxt) & (nxt < c))
        bi = np.arange(c)[:, None] // s
        bj = np.arange(c)[None, :] // s
        masks.append((bi % 2 == 1) & (bj == bi - 1))
    fwd_out = np.concatenate([incl] + mq + mk, axis=0).astype(np.float32)
    fwd_scan = np.concatenate([after, np.ones((8, c), bool)], axis=0).astype(np.float32)
    fwd_mask = np.stack(masks).astype(np.float32)

    def flip_rows(m):
        return m.reshape(-1, c, c)[:, ::-1, ::-1].reshape(m.shape)

    out_m = np.stack([fwd_out, flip_rows(fwd_out)])
    scan_m = np.stack([fwd_scan, np.concatenate([flip_rows(fwd_scan[:c]), fwd_scan[c:]], axis=0)])
    mask_m = np.stack([fwd_mask, fwd_mask[:, ::-1, ::-1]])
    mask_m = np.concatenate([mask_m, mask_m], axis=2)
    tile3 = lambda m: np.concatenate([m, m, m], axis=-1)
    return (jnp.asarray(tile3(out_m), BF16), jnp.asarray(tile3(scan_m), BF16), jnp.asarray(mask_m, F32))


def _split3(g):
    g1 = g.astype(BF16)
    r1 = g - g1.astype(F32)
    g2 = r1.astype(BF16)
    g3 = (r1 - g2.astype(F32)).astype(BF16)
    return jnp.concatenate([g1, g2, g3], axis=0)


def _gates_kernel(u_ref, w_ref, b_ref, o_ref):
    u = u_ref[0]
    for d in range(2):
        for p in range(GLA_PAIRS):
            pre = _dot(u, w_ref[d, p]) + b_ref[d, p]
            logsig = jnp.minimum(pre, 0.0) - jnp.log(1.0 + jnp.exp(-jnp.abs(pre)))
            o_ref[d, p] = logsig * (1.0 / GLA_GATE_NORM)


def _gla_gates(z, wg, bg, *, tm):
    t = z.shape[1]
    return pl.pallas_call(
        _gates_kernel,
        grid=(t // tm,),
        in_specs=[pl.BlockSpec((1, tm, LANES), lambda i: (ZB_GATE, i, 0)),
                  pl.BlockSpec((2, GLA_PAIRS, LANES, LANES), lambda i: (0, 0, 0, 0)),
                  pl.BlockSpec((2, GLA_PAIRS, 1, LANES), lambda i: (0, 0, 0, 0))],
        out_specs=pl.BlockSpec((2, GLA_PAIRS, tm, LANES), lambda i: (0, 0, i, 0)),
        out_shape=jax.ShapeDtypeStruct((2, GLA_PAIRS, t, LANES), F32),
        compiler_params=_cparams(("parallel",)),
        name="gla_gates",
    )(z, wg, bg)


def _pair_lane_mask(shape):
    return lax.broadcasted_iota(jnp.int32, shape, 1) < GLA_DK


def _gla_scan_kernel(k_ref, v_ref, g_ref, m_ref, s0_ref, sprev_ref, sfin_ref, s_acc, *, chunks):
    d = pl.program_id(2)
    c = GLA_CHUNK

    @pl.when(pl.program_id(3) == 0)
    def _():
        s_acc[...] = s0_ref[0, 0, 0]

    low = _pair_lane_mask((LANES, LANES))

    def body(jj, carry):
        cc = jj + d * (chunks - 1 - 2 * jj)
        rows = pl.ds(pl.multiple_of(cc * c, c), c)
        e = _dot(m_ref[0], _split3(g_ref[0, 0, rows, :]))
        khat = (k_ref[0, rows, :].astype(F32) * jnp.exp(e[0:c])).astype(BF16)
        vv = jnp.concatenate([v_ref[0, rows, :], v_ref[1, rows, :]], axis=1)
        full = _dot_tn(vv, khat)
        upd = jnp.where(low, full[0:LANES], full[LANES:2 * LANES])
        s_old = s_acc[...]
        sprev_ref[0, 0, 0, cc] = s_old.astype(BF16)
        s_acc[...] = jnp.exp(e[c:c + 1]) * s_old + upd
        return carry

    lax.fori_loop(0, chunks, body, 0, unroll=4)
    sfin_ref[0, 0, 0] = s_acc[...]


def _gla_scan(z, gates, scan_m, s0, *, batch, seq, tq):
    nt = seq // tq
    chunks = tq // GLA_CHUNK
    nc = seq // GLA_CHUNK
    tile = lambda d, i: i + d * (nt - 1 - 2 * i)
    return pl.pallas_call(
        functools.partial(_gla_scan_kernel, chunks=chunks),
        grid=(batch, GLA_PAIRS, 2, nt),
        in_specs=[pl.BlockSpec((1, tq, LANES), lambda b, p, d, i: (ZB_KG + p, b * nt + tile(d, i), 0)),
                  pl.BlockSpec((2, tq, LANES), lambda b, p, d, i: (ZB_VG // 2 + p, b * nt + tile(d, i), 0)),
                  pl.BlockSpec((1, 1, tq, LANES), lambda b, p, d, i: (d, p, b * nt + tile(d, i), 0)),
                  pl.BlockSpec((1,) + scan_m.shape[1:], lambda b, p, d, i: (d, 0, 0)),
                  pl.BlockSpec((1, 1, 1, LANES, LANES), lambda b, p, d, i: (b, p, d, 0, 0))],
        out_specs=[pl.BlockSpec((1, 1, 1, chunks, LANES, LANES), lambda b, p, d, i: (b, p, d, tile(d, i), 0, 0)),
                   pl.BlockSpec((1, 1, 1, LANES, LANES), lambda b, p, d, i: (b, p, d, 0, 0))],
        out_shape=[jax.ShapeDtypeStruct((batch, GLA_PAIRS, 2, nc, LANES, LANES), BF16),
                   jax.ShapeDtypeStruct((batch, GLA_PAIRS, 2, LANES, LANES), F32)],
        scratch_shapes=[pltpu.VMEM((LANES, LANES), F32)],
        compiler_params=_cparams(("parallel", "parallel", "parallel", "arbitrary")),
        name="gla_scan",
    )(z, z, gates, scan_m, s0)


def _gla_out_kernel(q_ref, k_ref, v_ref, r_ref, g_ref, sprev_ref, m_ref, mask_ref, gain_ref, o_ref, *, chunks):
    c = GLA_CHUNK
    low = _pair_lane_mask((c, LANES))

    def stack_heads(a):
        return jnp.concatenate([jnp.where(low, a, 0.0), jnp.where(low, 0.0, a)], axis=0).astype(BF16)

    def body(cc, carry):
        rows = pl.ds(pl.multiple_of(cc * c, c), c)
        kb = k_ref[0, rows, :]
        q = q_ref[0, rows, :].astype(F32)
        k = kb.astype(F32)
        vv = jnp.concatenate([v_ref[0, rows, :], v_ref[1, rows, :]], axis=1)
        o0 = jnp.zeros((c, LANES), F32)
        o1 = jnp.zeros((c, LANES), F32)
        for d in range(2):
            e = _dot(m_ref[d], _split3(g_ref[d, 0, rows, :]))
            inter = _dot_nt(stack_heads(q * jnp.exp(e[0:c])), sprev_ref[0, 0, d, cc])
            a = mask_ref[d, 0] * _dot_nt(stack_heads(q), kb)
            n_q = len(GLA_LEVELS) - 1
            for idx in range(len(GLA_LEVELS)):
                ek = e[(1 + n_q + idx) * c:(2 + n_q + idx) * c]
                qt = q * jnp.exp(e[(1 + idx) * c:(2 + idx) * c]) if idx < n_q else q
                kt = (k * jnp.exp(ek)).astype(BF16)
                a = a + mask_ref[d, 1 + idx] * _dot_nt(stack_heads(qt), kt)
            ov = _dot(a.astype(BF16), vv)
            o0 = o0 + ov[0:c, 0:LANES] + inter[0:c]
            o1 = o1 + ov[c:2 * c, LANES:2 * LANES] + inter[c:2 * c]
        for h, o in enumerate((o0, o1)):
            of = o * lax.rsqrt(jnp.mean(o * o, axis=-1, keepdims=True) + EPS) * gain_ref[...]
            r = r_ref[h, rows, :].astype(F32)
            o_ref[rows, h * LANES:(h + 1) * LANES] = (of * (r / (1.0 + jnp.exp(-r)))).astype(BF16)
        return carry

    lax.fori_loop(0, chunks, body, 0, unroll=2)


def _gla_output(z, gates, sprev, out_m, mask_m, gain, *, batch, seq, tq):
    nt = seq // tq
    chunks = tq // GLA_CHUNK
    return pl.pallas_call(
        functools.partial(_gla_out_kernel, chunks=chunks),
        grid=(batch, GLA_PAIRS, nt),
        in_specs=[pl.BlockSpec((1, tq, LANES), lambda b, p, i: (ZB_QG + p, b * nt + i, 0)),
                  pl.BlockSpec((1, tq, LANES), lambda b, p, i: (ZB_KG + p, b * nt + i, 0)),
                  pl.BlockSpec((2, tq, LANES), lambda b, p, i: (ZB_VG // 2 + p, b * nt + i, 0)),
                  pl.BlockSpec((2, tq, LANES), lambda b, p, i: (ZB_RG // 2 + p, b * nt + i, 0)),
                  pl.BlockSpec((2, 1, tq, LANES), lambda b, p, i: (0, p, b * nt + i, 0)),
                  pl.BlockSpec((1, 1, 2, chunks, LANES, LANES), lambda b, p, i: (b, p, 0, i, 0, 0)),
                  pl.BlockSpec(out_m.shape, lambda b, p, i: (0, 0, 0)),
                  pl.BlockSpec(mask_m.shape, lambda b, p, i: (0, 0, 0, 0)),
                  pl.BlockSpec((1, LANES), lambda b, p, i: (0, 0))],
        out_specs=pl.BlockSpec((tq, 2 * LANES), lambda b, p, i: (b * nt + i, p)),
        out_shape=jax.ShapeDtypeStruct((batch * seq, GLA_HEADS * LANES), BF16),
        compiler_params=_cparams(("parallel", "parallel", "parallel")),
        name="gla_output",
    )(z, z, z, z, gates, sprev, out_m, mask_m, gain.reshape(1, LANES))


def _outproj_kernel(ya_ref, yb_ref, ys_ref, w_ref, x_ref, mod_ref, g_ref, x1_ref, h2_ref):
    na = ya_ref.shape[1]
    nb = yb_ref.shape[1]
    proj = (_dot(ya_ref[...], w_ref[0:na, :]) + _dot(yb_ref[...], w_ref[na:na + nb, :])
            + _dot(ys_ref[...], w_ref[na + nb:, :]))
    x1 = x_ref[...] + mod_ref[0, 2:3, :] * proj
    x1_ref[...] = x1
    ms = jnp.mean(x1 * x1, axis=-1, keepdims=True)
    y = x1 * lax.rsqrt(ms + EPS) * g_ref[...]
    h2_ref[...] = (y * (1.0 + mod_ref[0, 4:5, :]) + mod_ref[0, 3:4, :]).astype(BF16)


def _outproj(ya, yb, ys, w_out, xt, mod, gain, *, tm, tiles_per_batch, ctx_row):
    t, d = xt.shape
    mod_map = (lambda i: (i // tiles_per_batch, 0, 0)) if ctx_row is None else (lambda i: (ctx_row, 0, 0))
    row = lambda i: (i, 0)
    return pl.pallas_call(
        _outproj_kernel,
        grid=(t // tm,),
        in_specs=[pl.BlockSpec((tm, ya.shape[1]), row),
                  pl.BlockSpec((tm, yb.shape[1]), row),
                  pl.BlockSpec((tm, ys.shape[1]), row),
                  pl.BlockSpec(w_out.shape, lambda i: (0, 0), pipeline_mode=pl.Buffered(1)),
                  pl.BlockSpec((tm, d), row),
                  pl.BlockSpec((1, 6, d), mod_map),
                  pl.BlockSpec((1, d), lambda i: (0, 0))],
        out_specs=[pl.BlockSpec((tm, d), row), pl.BlockSpec((tm, d), row)],
        out_shape=[jax.ShapeDtypeStruct((t, d), F32), jax.ShapeDtypeStruct((t, d), BF16)],
        compiler_params=_cparams(("parallel",)),
        name="out_proj",
    )(ya, yb, ys, w_out, xt, mod, gain.reshape(1, d))


def _mlp_kernel(h_ref, w1_ref, w2_ref, x_ref, mod_ref, g_ref, o_ref, acc_ref, *, final_norm):
    f = pl.program_id(1)

    @pl.when(f == 0)
    def _():
        acc_ref[...] = jnp.zeros_like(acc_ref)

    a = jnp.maximum(_dot(h_ref[...], w1_ref[...]), 0.0)
    acc_ref[...] += _dot((a * a).astype(BF16), w2_ref[...])

    @pl.when(f == pl.num_programs(1) - 1)
    def _():
        out = x_ref[...] + mod_ref[0, 5:6, :] * acc_ref[...]
        if final_norm:
            ms = jnp.mean(out * out, axis=-1, keepdims=True)
            out = out * lax.rsqrt(ms + EPS) * g_ref[...]
        o_ref[...] = out


def _mlp(h2, w1, w2, x1, mod, final_gain, *, tm, tf, tiles_per_batch, ctx_row, final_norm):
    t, d = x1.shape
    ff = w1.shape[1]
    mod_map = (lambda i, f: (i // tiles_per_batch, 0, 0)) if ctx_row is None else (lambda i, f: (ctx_row, 0, 0))
    return pl.pallas_call(
        functools.partial(_mlp_kernel, final_norm=final_norm),
        grid=(t // tm, ff // tf),
        in_specs=[pl.BlockSpec((tm, d), lambda i, f: (i, 0)),
                  pl.BlockSpec((d, tf), lambda i, f: (0, f)),
                  pl.BlockSpec((tf, d), lambda i, f: (f, 0)),
                  pl.BlockSpec((tm, d), lambda i, f: (i, 0)),
                  pl.BlockSpec((1, 6, d), mod_map),
                  pl.BlockSpec((1, d), lambda i, f: (0, 0))],
        out_specs=pl.BlockSpec((tm, d), lambda i, f: (i, 0)),
        out_shape=jax.ShapeDtypeStruct((t, d), F32),
        scratch_shapes=[pltpu.VMEM((tm, d), F32)],
        compiler_params=_cparams(("parallel", "arbitrary")),
        name="mlp",
    )(h2, w1, w2, x1, mod, final_gain.reshape(1, d))


def _rope_tables(seq):
    t = jnp.arange(seq, dtype=jnp.int32)
    row = (t // GRID_W).astype(F32)
    col = (t % GRID_W).astype(F32)
    n_freq = HEAD_DIM // 4
    inv = ROPE_THETA ** (-jnp.arange(n_freq, dtype=F32) / n_freq)
    ar = row[:, None] * inv[None]
    ac = col[:, None] * inv[None]
    cos = jnp.concatenate([jnp.cos(ar), jnp.cos(ar), jnp.cos(ac), jnp.cos(ac)], axis=-1)
    sin = jnp.concatenate([-jnp.sin(ar), jnp.sin(ar), -jnp.sin(ac), jnp.sin(ac)], axis=-1)
    return cos, sin


def _reorder_w_in(w_in):
    d = w_in.shape[0]
    g0 = (ZB_QS - 0) * LANES
    g1 = g0 + 2 * GLA_GATE_RANK
    pad = jnp.zeros((d, LANES - 2 * GLA_GATE_RANK), w_in.dtype)
    return jnp.concatenate([w_in[:, :g0], w_in[:, g1:], w_in[:, g0:g1], pad], axis=1).astype(BF16)


def _gate_weights(wg_f, bg_f, wg_b, bg_b):
    r = GLA_GATE_RANK
    w = jnp.zeros((2, GLA_PAIRS, LANES, LANES), F32)
    for p in range(GLA_PAIRS):
        w = w.at[0, p, 0:r, :].set(wg_f[:, p * LANES:(p + 1) * LANES])
        w = w.at[1, p, r:2 * r, :].set(wg_b[:, p * LANES:(p + 1) * LANES])
    b = jnp.stack([bg_f.reshape(GLA_PAIRS, 1, LANES), bg_b.reshape(GLA_PAIRS, 1, LANES)])
    return w.astype(BF16), b.astype(F32)


@jax.jit
def _forward(x, c, ctx, c_ctx, w_mod, b_mod, norm1_g, norm2_g, w_in, na_rpb, gla_wg_fwd, gla_bg_fwd,
             gla_wg_bwd, gla_bg_bwd, gla_norm_g, swa_sink, w_out, w_ff1, w_ff2, final_norm_g):
    batch, seq, d = x.shape
    n_ctx = ctx.shape[1]
    depth = w_mod.shape[0]
    ctx_row = batch
    tm = 512
    tpb = seq // tm

    cond = jnp.concatenate([c, c_ctx[None], jnp.zeros((8 - batch - 1, d), F32)], axis=0)
    mods = _modulation(cond, w_mod, b_mod).reshape(depth, 8, 6, d)
    cos, sin = _rope_tables(seq)
    out_m, scan_m, mask_m = _gla_matrices()
    zero_state = jnp.zeros((batch, GLA_PAIRS, 2, LANES, LANES), F32)

    xl = x.reshape(batch * seq, d)
    xc = ctx.reshape(batch * n_ctx, d)
    for i in range(depth):
        with_ctx = i < depth - 1
        mod = mods[i]
        w_in_r = _reorder_w_in(w_in[i])
        w_out_b = w_out[i].astype(BF16)
        w1 = w_ff1[i].astype(BF16)
        w2 = w_ff2[i].astype(BF16)
        wg, bg = _gate_weights(gla_wg_fwd[i], gla_bg_fwd[i], gla_wg_bwd[i], gla_bg_bwd[i])
        bias = _na_bias_tables(na_rpb[i], seq // GRID_W)

        z = _inproj(xl, mod, norm1_g[i], w_in_r, cos, sin, tm=tm, tiles_per_batch=tpb, ctx_row=None)
        zc = _inproj(xc, mod, norm1_g[i], w_in_r, cos, sin, tm=n_ctx, tiles_per_batch=1, ctx_row=ctx_row)

        gates_c = _gla_gates(zc, wg, bg, tm=n_ctx)
        gates_l = _gla_gates(z, wg, bg, tm=min(2048, seq))
        sprev_c, sfin_c = _gla_scan(zc, gates_c, scan_m, zero_state, batch=batch, seq=n_ctx, tq=n_ctx)
        sprev_l, _ = _gla_scan(z, gates_l, scan_m, sfin_c, batch=batch, seq=seq, tq=min(2048, seq))

        ya = _neighbourhood(z, zc, bias, batch=batch, seq=seq, n_ctx=n_ctx)
        yb = _gla_output(z, gates_l, sprev_l, out_m, mask_m, gla_norm_g[i], batch=batch, seq=seq, tq=512)
        ys = _window_attention(z, zc, swa_sink[i], batch=batch, seq=seq, n_ctx=n_ctx, tq=min(1024, seq))

        x1, h2 = _outproj(ya, yb, ys, w_out_b, xl, mod, norm2_g[i], tm=tm, tiles_per_batch=tpb, ctx_row=None)
        xl = _mlp(h2, w1, w2, x1, mod, final_norm_g, tm=tm, tf=1024, tiles_per_batch=tpb, ctx_row=None,
                  final_norm=not with_ctx)

        if with_ctx:
            ya_c = _ctx_attention(zc, swa_sink[i], batch=batch, n_ctx=n_ctx, q_blk=ZB_QA, k_blk=ZB_KA,
                                  v_blk=ZB_VA, heads=NA_HEADS, group=1, use_sink=False)
            yb_c = _gla_output(zc, gates_c, sprev_c, out_m, mask_m, gla_norm_g[i], batch=batch, seq=n_ctx,
                               tq=n_ctx)
            ys_c = _ctx_attention(zc, swa_sink[i], batch=batch, n_ctx=n_ctx, q_blk=ZB_QS, k_blk=ZB_KS,
                                  v_blk=ZB_VS, heads=SWA_Q_HEADS, group=SWA_GROUP, use_sink=True)
            x1c, h2c = _outproj(ya_c, yb_c, ys_c, w_out_b, xc, mod, norm2_g[i], tm=n_ctx, tiles_per_batch=1,
                                ctx_row=ctx_row)
            xc = _mlp(h2c, w1, w2, x1c, mod, final_norm_g, tm=n_ctx, tf=1024, tiles_per_batch=1,
                      ctx_row=ctx_row, final_norm=False)
    return xl.reshape(batch, seq, d)


def kernel(x, c, ctx, c_ctx, w_mod, b_mod, norm1_g, norm2_g, w_in, na_rpb, gla_wg_fwd, gla_bg_fwd, gla_wg_bwd,
           gla_bg_bwd, gla_norm_g, swa_sink, w_out, w_ff1, w_ff2, final_norm_g):
    return _forward(x, c, ctx, c_ctx, w_mod, b_mod, norm1_g, norm2_g, w_in, na_rpb, gla_wg_fwd, gla_bg_fwd,
                    gla_wg_bwd, gla_bg_bwd, gla_norm_g, swa_sink, w_out, w_ff1, w_ff2, final_norm_g)
```

```python
import functools

import numpy as np
import jax
import jax.numpy as jnp
from jax import lax
from jax.experimental import pallas as pl
from jax.experimental.pallas import tpu as pltpu

F32 = jnp.float32
BF16 = jnp.bfloat16

GRID_W = 64
HEAD_DIM = 128
LANES = 128
NA_HEADS = 4
NA_KH = 8
NA_KW = 16
GLA_HEADS = 4
GLA_DK = 64
GLA_GATE_RANK = 16
GLA_GATE_NORM = 16.0
GLA_CHUNK = 64
GLA_PAIRS = GLA_HEADS // 2
SWA_Q_HEADS = 8
SWA_KV_HEADS = 2
SWA_GROUP = SWA_Q_HEADS // SWA_KV_HEADS
SWA_WINDOW = 128
SWA_BLOCK = 128
ROPE_THETA = 10000.0
EPS = 1e-6
NEG_INF = -1e30

ZB_QA, ZB_KA, ZB_VA = 0, 4, 8
ZB_QG, ZB_KG, ZB_VG, ZB_RG = 12, 14, 16, 20
ZB_QS, ZB_KS, ZB_VS = 24, 32, 34
ZB_GATE = 36
N_ZB = 37
ROPE_BLOCKS = tuple(range(ZB_QS, ZB_VS))
ATTN_SCALE = HEAD_DIM ** -0.5
Q_SCALE = {**{j: ATTN_SCALE for j in range(ZB_QA, ZB_KA)},
           **{j: GLA_DK ** -0.5 for j in range(ZB_QG, ZB_KG)},
           **{j: ATTN_SCALE for j in range(ZB_QS, ZB_KS)}}

NA_GROUP_ROWS = 4
NA_KEY_ROWS = 12
GLA_LEVELS = (32, 16, 8, 4, 2, 1)

VMEM_LIMIT = 58 * 1024 * 1024


def _cparams(sem):
    return pltpu.CompilerParams(dimension_semantics=sem, vmem_limit_bytes=VMEM_LIMIT)


def _dot(a, b):
    return jnp.dot(a, b, preferred_element_type=F32)


def _dot_nt(a, b):
    return lax.dot_general(a, b, (((1,), (1,)), ((), ())), preferred_element_type=F32)


def _dot_tn(a, b):
    return lax.dot_general(a, b, (((0,), (0,)), ((), ())), preferred_element_type=F32)


def _mod_kernel(a_ref, w_ref, b_ref, o_ref):
    a = a_ref[...]
    act = a / (1.0 + jnp.exp(-a))
    o_ref[0] = _dot(act.astype(BF16), w_ref[0].astype(BF16)) + b_ref[0]


def _modulation(cond_rows, w_mod, b_mod):
    depth, d, n = w_mod.shape
    tn = 1024
    return pl.pallas_call(
        _mod_kernel,
        grid=(depth, n // tn),
        in_specs=[pl.BlockSpec((8, d), lambda l, j: (0, 0)),
                  pl.BlockSpec((1, d, tn), lambda l, j: (l, 0, j)),
                  pl.BlockSpec((1, 1, tn), lambda l, j: (l, 0, j))],
        out_specs=pl.BlockSpec((1, 8, tn), lambda l, j: (l, 0, j)),
        out_shape=jax.ShapeDtypeStruct((depth, 8, n), F32),
        compiler_params=_cparams(("parallel", "parallel")),
        name="modulation",
    )(cond_rows, w_mod, b_mod.reshape(depth, 1, n))


def _rope_swap(a):
    lane = lax.broadcasted_iota(jnp.int32, a.shape, 1)
    first = (lane % 64) < 32
    return jnp.where(first, pltpu.roll(a, 96, 1), pltpu.roll(a, 32, 1))


def _inproj_kernel(x_ref, mod_ref, g_ref, w_ref, cos_ref, sin_ref, o_ref, *, rope, chunk):
    x = x_ref[...]
    ms = jnp.mean(x * x, axis=-1, keepdims=True)
    y = x * lax.rsqrt(ms + EPS) * g_ref[...]
    h = (y * (1.0 + mod_ref[0, 1:2, :]) + mod_ref[0, 0:1, :]).astype(BF16)
    for j0 in range(0, N_ZB, chunk):
        nblk = min(chunk, N_ZB - j0)
        acc = _dot(h, w_ref[:, j0 * LANES:(j0 + nblk) * LANES])
        for jj in range(nblk):
            j = j0 + jj
            a = acc[:, jj * LANES:(jj + 1) * LANES]
            if j in Q_SCALE:
                a = a * Q_SCALE[j]
            if rope and j in ROPE_BLOCKS:
                a = a * cos_ref[...] + _rope_swap(a) * sin_ref[...]
            o_ref[j] = a.astype(BF16)


def _inproj(xt, mod, gain, w, cos, sin, *, tm, tiles_per_batch, ctx_row):
    t, d = xt.shape
    rope = ctx_row is None
    if rope:
        mod_map = lambda i: (i // tiles_per_batch, 0, 0)
        pos_map = lambda i: (i % tiles_per_batch, 0)
    else:
        mod_map = lambda i: (ctx_row, 0, 0)
        pos_map = lambda i: (0, 0)
    return pl.pallas_call(
        functools.partial(_inproj_kernel, rope=rope, chunk=4),
        grid=(t // tm,),
        in_specs=[pl.BlockSpec((tm, d), lambda i: (i, 0)),
                  pl.BlockSpec((1, 6, d), mod_map),
                  pl.BlockSpec((1, d), lambda i: (0, 0)),
                  pl.BlockSpec((d, N_ZB * LANES), lambda i: (0, 0), pipeline_mode=pl.Buffered(1)),
                  pl.BlockSpec((tm, LANES), pos_map),
                  pl.BlockSpec((tm, LANES), pos_map)],
        out_specs=pl.BlockSpec((N_ZB, tm, LANES), lambda i: (0, i, 0)),
        out_shape=jax.ShapeDtypeStruct((N_ZB, t, LANES), BF16),
        compiler_params=_cparams(("parallel",)),
        name="in_proj",
    )(xt, mod, gain.reshape(1, d), w, cos, sin)


def _na_kernel(q_ref, k_ref, v_ref, kc_ref, vc_ref, bias_ref, o_ref, *, n_groups):
    gq = NA_GROUP_ROWS * GRID_W
    gk = NA_KEY_ROWS * GRID_W
    max_base = n_groups * NA_GROUP_ROWS - NA_KEY_ROWS
    kc = kc_ref[0]
    vc = vc_ref[0]

    def body(g, carry):
        base = jnp.clip(g * NA_GROUP_ROWS - NA_KH // 2, 0, max_base)
        kstart = pl.multiple_of(base * GRID_W, GRID_W)
        qstart = pl.multiple_of(g * gq, gq)
        q = q_ref[0, pl.ds(qstart, gq), :]
        k = k_ref[0, pl.ds(kstart, gk), :]
        v = v_ref[0, pl.ds(kstart, gk), :]
        sel = jnp.where(g == 0, 0, jnp.where(g == n_groups - 1, 2, 1))
        s_loc = _dot_nt(q, k) + bias_ref[0, sel]
        s_ctx = _dot_nt(q, kc)
        m = jnp.maximum(jnp.max(s_loc, axis=-1, keepdims=True), jnp.max(s_ctx, axis=-1, keepdims=True))
        p_loc = jnp.exp(s_loc - m)
        p_ctx = jnp.exp(s_ctx - m)
        l = jnp.sum(p_loc, axis=-1, keepdims=True) + jnp.sum(p_ctx, axis=-1, keepdims=True)
        o = _dot(p_loc.astype(BF16), v) + _dot(p_ctx.astype(BF16), vc)
        o_ref[pl.ds(qstart, gq), :] = (o / l).astype(BF16)
        return carry

    lax.fori_loop(0, n_groups, body, 0, unroll=2)


def _na_bias_tables(rpb, rows):
    kh = NA_KH
    heads = rpb.shape[0]
    c = np.arange(GRID_W)[:, None]
    kc = np.arange(GRID_W)[None, :]
    cs = np.clip(c - NA_KW // 2, 0, GRID_W - NA_KW)
    col_ok = (kc >= cs) & (kc < cs + NA_KW)
    d = np.arange(2 * NA_KW - 1)[:, None, None]
    onehot = ((kc - c + NA_KW - 1)[None] == d) & col_ok[None]
    toep = jnp.einsum('hrd,dck->hrck', rpb.astype(F32), jnp.asarray(onehot, F32),
                      precision=lax.Precision.HIGHEST)
    toep = toep + jnp.asarray(np.where(col_ok, 0.0, NEG_INF), F32)
    dead = jnp.full((heads, GRID_W, GRID_W), NEG_INF, F32)
    tabs = []
    for r0, base in ((0, 0), (kh // 2, 0), (rows - NA_GROUP_ROWS, rows - NA_KEY_ROWS)):
        blk_rows = []
        for rq in range(NA_GROUP_ROWS):
            r = r0 + rq
            rs = min(max(r - kh // 2, 0), rows - kh)
            blks = []
            for ki in range(NA_KEY_ROWS):
                krow = base + ki
                blks.append(toep[:, krow - r + NA_KH - 1] if rs <= krow < rs + kh else dead)
            blk_rows.append(jnp.concatenate(blks, axis=-1))
        tabs.append(jnp.concatenate(blk_rows, axis=-2))
    return jnp.stack(tabs, axis=1)


def _neighbourhood(z, zc, bias, *, batch, seq, n_ctx):
    rows = seq // GRID_W
    n_groups = rows // NA_GROUP_ROWS
    gq = NA_GROUP_ROWS * GRID_W
    gk = NA_KEY_ROWS * GRID_W
    return pl.pallas_call(
        functools.partial(_na_kernel, n_groups=n_groups),
        grid=(batch, NA_HEADS),
        in_specs=[pl.BlockSpec((1, seq, LANES), lambda b, h: (ZB_QA + h, b, 0)),
                  pl.BlockSpec((1, seq, LANES), lambda b, h: (ZB_KA + h, b, 0)),
                  pl.BlockSpec((1, seq, LANES), lambda b, h: (ZB_VA + h, b, 0)),
                  pl.BlockSpec((1, n_ctx, LANES), lambda b, h: (ZB_KA + h, b, 0)),
                  pl.BlockSpec((1, n_ctx, LANES), lambda b, h: (ZB_VA + h, b, 0)),
                  pl.BlockSpec((1, 3, gq, gk), lambda b, h: (h, 0, 0, 0))],
        out_specs=pl.BlockSpec((seq, LANES), lambda b, h: (b, h)),
        out_shape=jax.ShapeDtypeStruct((batch * seq, NA_HEADS * LANES), BF16),
        compiler_params=_cparams(("parallel", "parallel")),
        name="neighbourhood_attn",
    )(z, z, z, zc, zc, bias)


def _swa_kernel(sink_ref, q_ref, k_ref, v_ref, kc_ref, vc_ref, mask_ref, o_ref, *, blocks_per_step, n_blocks):
    grp = pl.program_id(1)
    step = pl.program_id(2)
    band = 3 * SWA_BLOCK
    kc = kc_ref[0]
    vc = vc_ref[0]

    def body(j, carry):
        n = step * blocks_per_step + j
        start = pl.multiple_of(jnp.clip((n - 1) * SWA_BLOCK, 0, (n_blocks - 3) * SWA_BLOCK), SWA_BLOCK)
        qs = pl.multiple_of(j * SWA_BLOCK, SWA_BLOCK)
        q = q_ref[:, pl.ds(qs, SWA_BLOCK), :].reshape(SWA_GROUP * SWA_BLOCK, LANES)
        k = k_ref[0, pl.ds(start, band), :]
        v = v_ref[0, pl.ds(start, band), :]
        sel = jnp.where(n == 0, 0, jnp.where(n == n_blocks - 1, 2, 1))
        msk = mask_ref[sel]
        s_loc = _dot_nt(q, k)
        s_ctx = _dot_nt(q, kc)
        for hq in range(SWA_GROUP):
            sl = s_loc[hq * SWA_BLOCK:(hq + 1) * SWA_BLOCK] + msk
            sc = s_ctx[hq * SWA_BLOCK:(hq + 1) * SWA_BLOCK]
            snk = sink_ref[grp * SWA_GROUP + hq]
            m = jnp.maximum(jnp.maximum(jnp.max(sl, axis=-1, keepdims=True),
                                        jnp.max(sc, axis=-1, keepdims=True)), snk)
            p_l = jnp.exp(sl - m)
            p_c = jnp.exp(sc - m)
            l = (jnp.sum(p_l, axis=-1, keepdims=True) + jnp.sum(p_c, axis=-1, keepdims=True)
                 + jnp.exp(snk - m))
            o = _dot(p_l.astype(BF16), v) + _dot(p_c.astype(BF16), vc)
            o_ref[pl.ds(qs, SWA_BLOCK), hq * LANES:(hq + 1) * LANES] = (o / l).astype(BF16)
        return carry

    lax.fori_loop(0, blocks_per_step, body, 0, unroll=2)


def _swa_masks():
    iq = np.arange(SWA_BLOCK)[:, None]
    ik = np.arange(3 * SWA_BLOCK)[None, :]
    tabs = [np.where(np.abs(ik - off - iq) <= SWA_WINDOW, 0.0, NEG_INF) for off in (0, SWA_BLOCK, 2 * SWA_BLOCK)]
    return jnp.asarray(np.stack(tabs), F32)


def _window_attention(z, zc, sink, *, batch, seq, n_ctx, tq):
    n_blocks = seq // SWA_BLOCK
    steps = seq // tq
    return pl.pallas_call(
        functools.partial(_swa_kernel, blocks_per_step=tq // SWA_BLOCK, n_blocks=n_blocks),
        grid=(batch, SWA_KV_HEADS, steps),
        in_specs=[pl.BlockSpec(memory_space=pltpu.SMEM),
                  pl.BlockSpec((SWA_GROUP, tq, LANES), lambda b, g, i: (ZB_QS // SWA_GROUP + g, b * steps + i, 0)),
                  pl.BlockSpec((1, seq, LANES), lambda b, g, i: (ZB_KS + g, b, 0)),
                  pl.BlockSpec((1, seq, LANES), lambda b, g, i: (ZB_VS + g, b, 0)),
                  pl.BlockSpec((1, n_ctx, LANES), lambda b, g, i: (ZB_KS + g, b, 0)),
                  pl.BlockSpec((1, n_ctx, LANES), lambda b, g, i: (ZB_VS + g, b, 0)),
                  pl.BlockSpec((3, SWA_BLOCK, 3 * SWA_BLOCK), lambda b, g, i: (0, 0, 0))],
        out_specs=pl.BlockSpec((tq, SWA_GROUP * LANES), lambda b, g, i: (b * steps + i, g)),
        out_shape=jax.ShapeDtypeStruct((batch * seq, SWA_Q_HEADS * LANES), BF16),
        compiler_params=_cparams(("parallel", "parallel", "parallel")),
        name="window_attn",
    )(sink, z, z, z, zc, zc, _swa_masks())


def _ctx_attn_kernel(sink_ref, q_ref, k_ref, v_ref, o_ref, *, use_sink):
    q = q_ref[0]
    s = _dot_nt(q, k_ref[0])
    m = jnp.max(s, axis=-1, keepdims=True)
    if use_sink:
        snk = sink_ref[pl.program_id(1)]
        m = jnp.maximum(m, snk)
    p = jnp.exp(s - m)
    l = jnp.sum(p, axis=-1, keepdims=True)
    if use_sink:
        l = l + jnp.exp(snk - m)
    o_ref[...] = (_dot(p.astype(BF16), v_ref[0]) / l).astype(BF16)


def _ctx_attention(zc, sink, *, batch, n_ctx, q_blk, k_blk, v_blk, heads, group, use_sink):
    return pl.pallas_call(
        functools.partial(_ctx_attn_kernel, use_sink=use_sink),
        grid=(batch, heads),
        in_specs=[pl.BlockSpec(memory_space=pltpu.SMEM),
                  pl.BlockSpec((1, n_ctx, LANES), lambda b, h: (q_blk + h, b, 0)),
                  pl.BlockSpec((1, n_ctx, LANES), lambda b, h: (k_blk + h // group, b, 0)),
                  pl.BlockSpec((1, n_ctx, LANES), lambda b, h: (v_blk + h // group, b, 0))],
        out_specs=pl.BlockSpec((n_ctx, LANES), lambda b, h: (b, h)),
        out_shape=jax.ShapeDtypeStruct((batch * n_ctx, heads * LANES), BF16),
        compiler_params=_cparams(("parallel", "parallel")),
        name="context_attn",
    )(sink, zc, zc, zc)


def _gla_matrices():
    c = GLA_CHUNK
    i = np.arange(c)[:, None]
    t = np.arange(c)[None, :]
    incl = (t <= i)
    after = (t > i)
    lvl, masks = [], [(i == t)]
    for s in GLA_LEVELS:
        mid = (i // (2 * s)) * (2 * s) + s
        lvl.append(((i >= mid) & (t > mid) & (t <= i)) | ((i < mid) & (t > i) & (t <= mid)))
        bi = np.arange(c)[:, None] // s
        bj = np.arange(c)[None, :] // s
        masks.append((bi % 2 == 1) & (bj == bi - 1))
    fwd_out = np.concatenate([incl] + lvl, axis=0).astype(np.float32)
    fwd_scan = np.concatenate([after, np.ones((8, c), bool)], axis=0).astype(np.float32)
    fwd_mask = np.stack(masks).astype(np.float32)

    def flip_rows(m):
        return m.reshape(-1, c, c)[:, ::-1, ::-1].reshape(m.shape)

    out_m = np.stack([fwd_out, flip_rows(fwd_out)])
    scan_m = np.stack([fwd_scan, np.concatenate([flip_rows(fwd_scan[:c]), fwd_scan[c:]], axis=0)])
    mask_m = np.stack([fwd_mask, fwd_mask[:, ::-1, ::-1]])
    mask_m = np.concatenate([mask_m] * GLA_HEADS, axis=3)
    tile2 = lambda m: np.concatenate([m, m], axis=-1)
    return (jnp.asarray(tile2(out_m), BF16), jnp.asarray(tile2(scan_m), BF16), jnp.asarray(mask_m, F32))


def _split2(g):
    g1 = g.astype(BF16)
    g2 = (g - g1.astype(F32)).astype(BF16)
    return jnp.concatenate([g1, g2], axis=0)


def _gates_kernel(u_ref, w_ref, b_ref, o_ref):
    u = u_ref[0]
    for d in range(2):
        pre = _dot(u, w_ref[d]) + b_ref[d]
        logsig = jnp.minimum(pre, 0.0) - jnp.log(1.0 + jnp.exp(-jnp.abs(pre)))
        o_ref[d] = logsig * (1.0 / GLA_GATE_NORM)


def _gla_gates(z, wg, bg, *, tm):
    t = z.shape[1]
    kw = GLA_HEADS * GLA_DK
    return pl.pallas_call(
        _gates_kernel,
        grid=(t // tm,),
        in_specs=[pl.BlockSpec((1, tm, LANES), lambda i: (ZB_GATE, i, 0)),
                  pl.BlockSpec((2, LANES, kw), lambda i: (0, 0, 0)),
                  pl.BlockSpec((2, 1, kw), lambda i: (0, 0, 0))],
        out_specs=pl.BlockSpec((2, tm, kw), lambda i: (0, i, 0)),
        out_shape=jax.ShapeDtypeStruct((2, t, kw), F32),
        compiler_params=_cparams(("parallel",)),
        name="gla_gates",
    )(z, wg, bg)


def _head_of_lane(shape):
    return lax.broadcasted_iota(jnp.int32, shape, 1) // GLA_DK


def _cat_blocks(ref, rows):
    return jnp.concatenate([ref[j, rows, :] for j in range(ref.shape[0])], axis=1)


def _gla_scan_kernel(kf_ref, vf_ref, gf_ref, kb_ref, vb_ref, gb_ref, m_ref, s0_ref,
                     sprev_f_ref, sprev_b_ref, sfin_ref, s_acc, *, chunks):
    c = GLA_CHUNK

    @pl.when(pl.program_id(1) == 0)
    def _():
        s_acc[...] = s0_ref[0]

    head = _head_of_lane((LANES, GLA_HEADS * GLA_DK))

    def advance(d, k_ref, v_ref, g_ref, sprev_ref, cc):
        rows = pl.ds(pl.multiple_of(cc * c, c), c)
        e = _dot(m_ref[d], _split2(g_ref[0, rows, :]))
        khat = (_cat_blocks(k_ref, rows).astype(F32) * jnp.exp(e[0:c])).astype(BF16)
        full = _dot_tn(_cat_blocks(v_ref, rows), khat)
        upd = full[0:LANES]
        for h in range(1, GLA_HEADS):
            upd = jnp.where(head == h, full[h * LANES:(h + 1) * LANES], upd)
        s_old = s_acc[d]
        sprev_ref[0, cc] = s_old.astype(BF16)
        s_acc[d] = jnp.exp(e[c:c + 1]) * s_old + upd

    def body(jj, carry):
        advance(0, kf_ref, vf_ref, gf_ref, sprev_f_ref, jj)
        advance(1, kb_ref, vb_ref, gb_ref, sprev_b_ref, chunks - 1 - jj)
        return carry

    lax.fori_loop(0, chunks, body, 0, unroll=2)
    sfin_ref[0] = s_acc[...]


def _gla_scan(z, gates, scan_m, s0, *, batch, seq, tq):
    nt = seq // tq
    chunks = tq // GLA_CHUNK
    nc = seq // GLA_CHUNK
    kw = GLA_HEADS * GLA_DK
    fwd = lambda b, i: b * nt + i
    bwd = lambda b, i: b * nt + nt - 1 - i
    return pl.pallas_call(
        functools.partial(_gla_scan_kernel, chunks=chunks),
        grid=(batch, nt),
        in_specs=[pl.BlockSpec((2, tq, LANES), lambda b, i: (ZB_KG // 2, fwd(b, i), 0)),
                  pl.BlockSpec((4, tq, LANES), lambda b, i: (ZB_VG // 4, fwd(b, i), 0)),
                  pl.BlockSpec((1, tq, kw), lambda b, i: (0, fwd(b, i), 0)),
                  pl.BlockSpec((2, tq, LANES), lambda b, i: (ZB_KG // 2, bwd(b, i), 0)),
                  pl.BlockSpec((4, tq, LANES), lambda b, i: (ZB_VG // 4, bwd(b, i), 0)),
                  pl.BlockSpec((1, tq, kw), lambda b, i: (1, bwd(b, i), 0)),
                  pl.BlockSpec(scan_m.shape, lambda b, i: (0, 0, 0)),
                  pl.BlockSpec((1, 2, LANES, kw), lambda b, i: (b, 0, 0, 0))],
        out_specs=[pl.BlockSpec((1, chunks, LANES, kw), lambda b, i: (b, i, 0, 0)),
                   pl.BlockSpec((1, chunks, LANES, kw), lambda b, i: (b, nt - 1 - i, 0, 0)),
                   pl.BlockSpec((1, 2, LANES, kw), lambda b, i: (b, 0, 0, 0))],
        out_shape=[jax.ShapeDtypeStruct((batch, nc, LANES, kw), BF16),
                   jax.ShapeDtypeStruct((batch, nc, LANES, kw), BF16),
                   jax.ShapeDtypeStruct((batch, 2, LANES, kw), F32)],
        scratch_shapes=[pltpu.VMEM((2, LANES, kw), F32)],
        compiler_params=_cparams(("parallel", "arbitrary")),
        name="gla_scan",
    )(z, z, gates, z, z, gates, scan_m, s0)


def _gla_out_kernel(q_ref, k_ref, v_ref, r_ref, g_ref, sf_ref, sb_ref, m_ref, mask_ref, gain_ref, o_ref, *, chunks):
    c = GLA_CHUNK
    head = _head_of_lane((c, GLA_HEADS * GLA_DK))
    head_mask = [jnp.where(head == h, 1.0, 0.0).astype(BF16) for h in range(GLA_HEADS)]
    zero_blk = jnp.zeros((c, LANES), BF16)

    def stack_heads(ab):
        return jnp.concatenate([ab * head_mask[h] for h in range(GLA_HEADS)], axis=0)

    def body(cc, carry):
        rows = pl.ds(pl.multiple_of(cc * c, c), c)
        qb = _cat_blocks(q_ref, rows)
        kb = _cat_blocks(k_ref, rows)
        q = qb.astype(F32)
        k = kb.astype(F32)
        v_diag = jnp.concatenate(
            [jnp.concatenate([v_ref[h, rows, :] if j == h else zero_blk for j in range(GLA_HEADS)], axis=1)
             for h in range(GLA_HEADS)], axis=0)
        k_diag = stack_heads(kb)
        o = jnp.zeros((c, GLA_HEADS * LANES), F32)
        inters = []
        for d, s_ref in enumerate((sf_ref, sb_ref)):
            e = _dot(m_ref[d], _split2(g_ref[d, rows, :]))
            qhat = (q * jnp.exp(e[0:c])).astype(BF16)
            inters.append(_dot_nt(stack_heads(qhat), s_ref[0, cc]))
            a = mask_ref[d, 0] * _dot_nt(qb, k_diag)
            for idx in range(len(GLA_LEVELS)):
                x = jnp.exp(e[(1 + idx) * c:(2 + idx) * c])
                a = a + mask_ref[d, 1 + idx] * _dot_nt((q * x).astype(BF16), stack_heads((k * x).astype(BF16)))
            o = o + _dot(a.astype(BF16), v_diag)
        for h in range(GLA_HEADS):
            oh = o[:, h * LANES:(h + 1) * LANES] + inters[0][h * c:(h + 1) * c] + inters[1][h * c:(h + 1) * c]
            of = oh * lax.rsqrt(jnp.mean(oh * oh, axis=-1, keepdims=True) + EPS) * gain_ref[...]
            r = r_ref[h, rows, :].astype(F32)
            o_ref[rows, h * LANES:(h + 1) * LANES] = (of * (r / (1.0 + jnp.exp(-r)))).astype(BF16)
        return carry

    lax.fori_loop(0, chunks, body, 0, unroll=2)


def _gla_output(z, gates, sprev_f, sprev_b, out_m, mask_m, gain, *, batch, seq, tq):
    nt = seq // tq
    chunks = tq // GLA_CHUNK
    kw = GLA_HEADS * GLA_DK
    tile = lambda b, i: (b * nt + i)
    return pl.pallas_call(
        functools.partial(_gla_out_kernel, chunks=chunks),
        grid=(batch, nt),
        in_specs=[pl.BlockSpec((2, tq, LANES), lambda b, i: (ZB_QG // 2, tile(b, i), 0)),
                  pl.BlockSpec((2, tq, LANES), lambda b, i: (ZB_KG // 2, tile(b, i), 0)),
                  pl.BlockSpec((4, tq, LANES), lambda b, i: (ZB_VG // 4, tile(b, i), 0)),
                  pl.BlockSpec((4, tq, LANES), lambda b, i: (ZB_RG // 4, tile(b, i), 0)),
                  pl.BlockSpec((2, tq, kw), lambda b, i: (0, tile(b, i), 0)),
                  pl.BlockSpec((1, chunks, LANES, kw), lambda b, i: (b, i, 0, 0)),
                  pl.BlockSpec((1, chunks, LANES, kw), lambda b, i: (b, i, 0, 0)),
                  pl.BlockSpec(out_m.shape, lambda b, i: (0, 0, 0)),
                  pl.BlockSpec(mask_m.shape, lambda b, i: (0, 0, 0, 0)),
                  pl.BlockSpec((1, LANES), lambda b, i: (0, 0))],
        out_specs=pl.BlockSpec((tq, GLA_HEADS * LANES), lambda b, i: (tile(b, i), 0)),
        out_shape=jax.ShapeDtypeStruct((batch * seq, GLA_HEADS * LANES), BF16),
        compiler_params=_cparams(("parallel", "parallel")),
        name="gla_output",
    )(z, z, z, z, gates, sprev_f, sprev_b, out_m, mask_m, gain.reshape(1, LANES))


def _outproj_kernel(ya_ref, yb_ref, ys_ref, w_ref, x_ref, mod_ref, x1_ref):
    na = ya_ref.shape[1]
    nb = yb_ref.shape[1]
    proj = (_dot(ya_ref[...], w_ref[0:na, :]) + _dot(yb_ref[...], w_ref[na:na + nb, :])
            + _dot(ys_ref[...], w_ref[na + nb:, :]))
    x1_ref[...] = x_ref[...] + mod_ref[0, 2:3, :] * proj


def _outproj(ya, yb, ys, w_out, xt, mod, *, tm, tiles_per_batch, ctx_row):
    t, d = xt.shape
    mod_map = (lambda i: (i // tiles_per_batch, 0, 0)) if ctx_row is None else (lambda i: (ctx_row, 0, 0))
    row = lambda i: (i, 0)
    return pl.pallas_call(
        _outproj_kernel,
        grid=(t // tm,),
        in_specs=[pl.BlockSpec((tm, ya.shape[1]), row),
                  pl.BlockSpec((tm, yb.shape[1]), row),
                  pl.BlockSpec((tm, ys.shape[1]), row),
                  pl.BlockSpec(w_out.shape, lambda i: (0, 0), pipeline_mode=pl.Buffered(1)),
                  pl.BlockSpec((tm, d), row),
                  pl.BlockSpec((1, 6, d), mod_map)],
        out_specs=pl.BlockSpec((tm, d), row),
        out_shape=jax.ShapeDtypeStruct((t, d), F32),
        compiler_params=_cparams(("parallel",)),
        name="out_proj",
    )(ya, yb, ys, w_out, xt, mod)


def _mlp_kernel(x_ref, w1_ref, w2_ref, mod_ref, g2_ref, gf_ref, o_ref, h_ref, *, final_norm, n_split):
    f = pl.program_id(1)
    rows = x_ref.shape[0] // n_split

    @pl.when(f == 0)
    def _():
        for r in range(n_split):
            rs = slice(r * rows, (r + 1) * rows)
            x = x_ref[rs, :]
            ms = jnp.mean(x * x, axis=-1, keepdims=True)
            y = x * lax.rsqrt(ms + EPS) * g2_ref[...]
            h_ref[rs, :] = (y * (1.0 + mod_ref[0, 4:5, :]) + mod_ref[0, 3:4, :]).astype(BF16)
        o_ref[...] = jnp.zeros_like(o_ref)

    for r in range(n_split):
        rs = slice(r * rows, (r + 1) * rows)
        a = jnp.maximum(_dot(h_ref[rs, :], w1_ref[...]), 0.0)
        o_ref[rs, :] += _dot((a * a).astype(BF16), w2_ref[...])

    @pl.when(f == pl.num_programs(1) - 1)
    def _():
        out = x_ref[...] + mod_ref[0, 5:6, :] * o_ref[...]
        if final_norm:
            ms = jnp.mean(out * out, axis=-1, keepdims=True)
            out = out * lax.rsqrt(ms + EPS) * gf_ref[...]
        o_ref[...] = out


def _mlp(x1, w1, w2, mod, gain, final_gain, *, tm, tf, tiles_per_batch, ctx_row, final_norm):
    t, d = x1.shape
    ff = w1.shape[1]
    mod_map = (lambda i, f: (i // tiles_per_batch, 0, 0)) if ctx_row is None else (lambda i, f: (ctx_row, 0, 0))
    return pl.pallas_call(
        functools.partial(_mlp_kernel, final_norm=final_norm, n_split=max(1, tm // 512)),
        grid=(t // tm, ff // tf),
        in_specs=[pl.BlockSpec((tm, d), lambda i, f: (i, 0)),
                  pl.BlockSpec((d, tf), lambda i, f: (0, f)),
                  pl.BlockSpec((tf, d), lambda i, f: (f, 0)),
                  pl.BlockSpec((1, 6, d), mod_map),
                  pl.BlockSpec((1, d), lambda i, f: (0, 0)),
                  pl.BlockSpec((1, d), lambda i, f: (0, 0))],
        out_specs=pl.BlockSpec((tm, d), lambda i, f: (i, 0)),
        out_shape=jax.ShapeDtypeStruct((t, d), F32),
        scratch_shapes=[pltpu.VMEM((tm, d), BF16)],
        compiler_params=_cparams(("parallel", "arbitrary")),
        name="mlp",
    )(x1, w1, w2, mod, gain.reshape(1, d), final_gain.reshape(1, d))


def _rope_tables(seq):
    t = jnp.arange(seq, dtype=jnp.int32)
    row = (t // GRID_W).astype(F32)
    col = (t % GRID_W).astype(F32)
    n_freq = HEAD_DIM // 4
    inv = ROPE_THETA ** (-jnp.arange(n_freq, dtype=F32) / n_freq)
    ar = row[:, None] * inv[None]
    ac = col[:, None] * inv[None]
    cos = jnp.concatenate([jnp.cos(ar), jnp.cos(ar), jnp.cos(ac), jnp.cos(ac)], axis=-1)
    sin = jnp.concatenate([-jnp.sin(ar), jnp.sin(ar), -jnp.sin(ac), jnp.sin(ac)], axis=-1)
    return cos, sin


def _reorder_w_in(w_in):
    d = w_in.shape[0]
    g0 = (ZB_QS - 0) * LANES
    g1 = g0 + 2 * GLA_GATE_RANK
    pad = jnp.zeros((d, LANES - 2 * GLA_GATE_RANK), w_in.dtype)
    return jnp.concatenate([w_in[:, :g0], w_in[:, g1:], w_in[:, g0:g1], pad], axis=1).astype(BF16)


def _gate_weights(wg_f, bg_f, wg_b, bg_b):
    r = GLA_GATE_RANK
    kw = wg_f.shape[1]
    w = jnp.zeros((2, LANES, kw), F32)
    w = w.at[0, 0:r, :].set(wg_f)
    w = w.at[1, r:2 * r, :].set(wg_b)
    b = jnp.stack([bg_f.reshape(1, kw), bg_b.reshape(1, kw)])
    return w.astype(BF16), b.astype(F32)


@jax.jit
def _forward(x, c, ctx, c_ctx, w_mod, b_mod, norm1_g, norm2_g, w_in, na_rpb, gla_wg_fwd, gla_bg_fwd,
             gla_wg_bwd, gla_bg_bwd, gla_norm_g, swa_sink, w_out, w_ff1, w_ff2, final_norm_g):
    batch, seq, d = x.shape
    n_ctx = ctx.shape[1]
    depth = w_mod.shape[0]
    ctx_row = batch
    tm = 512
    tpb = seq // tm

    cond = jnp.concatenate([c, c_ctx[None], jnp.zeros((8 - batch - 1, d), F32)], axis=0)
    mods = _modulation(cond, w_mod, b_mod).reshape(depth, 8, 6, d)
    cos, sin = _rope_tables(seq)
    out_m, scan_m, mask_m = _gla_matrices()
    zero_state = jnp.zeros((batch, 2, LANES, GLA_HEADS * GLA_DK), F32)

    xl = x.reshape(batch * seq, d)
    xc = ctx.reshape(batch * n_ctx, d)
    for i in range(depth):
        with_ctx = i < depth - 1
        mod = mods[i]
        w_in_r = _reorder_w_in(w_in[i])
        w_out_b = w_out[i].astype(BF16)
        w1 = w_ff1[i].astype(BF16)
        w2 = w_ff2[i].astype(BF16)
        wg, bg = _gate_weights(gla_wg_fwd[i], gla_bg_fwd[i], gla_wg_bwd[i], gla_bg_bwd[i])
        bias = _na_bias_tables(na_rpb[i], seq // GRID_W)

        z = _inproj(xl, mod, norm1_g[i], w_in_r, cos, sin, tm=tm, tiles_per_batch=tpb, ctx_row=None)
        zc = _inproj(xc, mod, norm1_g[i], w_in_r, cos, sin, tm=n_ctx, tiles_per_batch=1, ctx_row=ctx_row)

        gates_c = _gla_gates(zc, wg, bg, tm=n_ctx)
        gates_l = _gla_gates(z, wg, bg, tm=min(2048, seq))
        spf_c, spb_c, sfin_c = _gla_scan(zc, gates_c, scan_m, zero_state, batch=batch, seq=n_ctx, tq=n_ctx)
        spf_l, spb_l, _ = _gla_scan(z, gates_l, scan_m, sfin_c, batch=batch, seq=seq, tq=min(2048, seq))

        ya = _neighbourhood(z, zc, bias, batch=batch, seq=seq, n_ctx=n_ctx)
        yb = _gla_output(z, gates_l, spf_l, spb_l, out_m, mask_m, gla_norm_g[i], batch=batch, seq=seq, tq=512)
        ys = _window_attention(z, zc, swa_sink[i], batch=batch, seq=seq, n_ctx=n_ctx, tq=min(1024, seq))

        x1 = _outproj(ya, yb, ys, w_out_b, xl, mod, tm=tm, tiles_per_batch=tpb, ctx_row=None)
        tm_mlp = min(1024, seq)
        xl = _mlp(x1, w1, w2, mod, norm2_g[i], final_norm_g, tm=tm_mlp, tf=512, tiles_per_batch=seq // tm_mlp,
                  ctx_row=None, final_norm=not with_ctx)

        if with_ctx:
            ya_c = _ctx_attention(zc, swa_sink[i], batch=batch, n_ctx=n_ctx, q_blk=ZB_QA, k_blk=ZB_KA,
                                  v_blk=ZB_VA, heads=NA_HEADS, group=1, use_sink=False)
            yb_c = _gla_output(zc, gates_c, spf_c, spb_c, out_m, mask_m, gla_norm_g[i], batch=batch, seq=n_ctx,
                               tq=n_ctx)
            ys_c = _ctx_attention(zc, swa_sink[i], batch=batch, n_ctx=n_ctx, q_blk=ZB_QS, k_blk=ZB_KS,
                                  v_blk=ZB_VS, heads=SWA_Q_HEADS, group=SWA_GROUP, use_sink=True)
            x1c = _outproj(ya_c, yb_c, ys_c, w_out_b, xc, mod, tm=n_ctx, tiles_per_batch=1, ctx_row=ctx_row)
            xc = _mlp(x1c, w1, w2, mod, norm2_g[i], final_norm_g, tm=n_ctx, tf=1024, tiles_per_batch=1,
                      ctx_row=ctx_row, final_norm=False)
    return xl.reshape(batch, seq, d)


def kernel(x, c, ctx, c_ctx, w_mod, b_mod, norm1_g, norm2_g, w_in, na_rpb, gla_wg_fwd, gla_bg_fwd, gla_wg_bwd,
           gla_bg_bwd, gla_norm_g, swa_sink, w_out, w_ff1, w_ff2, final_norm_g):
    return _forward(x, c, ctx, c_ctx, w_mod, b_mod, norm1_g, norm2_g, w_in, na_rpb, gla_wg_fwd, gla_bg_fwd,
                    gla_wg_bwd, gla_bg_bwd, gla_norm_g, swa_sink, w_out, w_ff1, w_ff2, final_norm_g)
```

```python
import functools

import numpy as np
import jax
import jax.numpy as jnp
from jax import lax
from jax.experimental import pallas as pl
from jax.experimental.pallas import tpu as pltpu

F32 = jnp.float32
BF16 = jnp.bfloat16

GRID_W = 64
HEAD_DIM = 128
LANES = 128
NA_HEADS = 4
NA_KH = 8
NA_KW = 16
GLA_HEADS = 4
GLA_DK = 64
GLA_GATE_RANK = 16
GLA_GATE_NORM = 16.0
GLA_CHUNK = 64
GLA_PAIRS = GLA_HEADS // 2
SWA_Q_HEADS = 8
SWA_KV_HEADS = 2
SWA_GROUP = SWA_Q_HEADS // SWA_KV_HEADS
SWA_WINDOW = 128
SWA_BLOCK = 128
ROPE_THETA = 10000.0
EPS = 1e-6
NEG_INF = -1e30

ZB_QA, ZB_KA, ZB_VA = 0, 4, 8
ZB_QG, ZB_KG, ZB_VG, ZB_RG = 12, 14, 16, 20
ZB_QS, ZB_KS, ZB_VS = 24, 32, 34
ZB_GATE = 36
N_ZB = 37
ROPE_BLOCKS = tuple(range(ZB_QS, ZB_VS))
LOG2E = 1.4426950408889634
ATTN_SCALE = HEAD_DIM ** -0.5 * LOG2E
Q_SCALE = {**{j: ATTN_SCALE for j in range(ZB_QA, ZB_KA)},
           **{j: GLA_DK ** -0.5 for j in range(ZB_QG, ZB_KG)},
           **{j: ATTN_SCALE for j in range(ZB_QS, ZB_KS)}}

NA_GROUP_ROWS = 4
NA_KEY_ROWS = 12
GLA_LEVELS = (32, 16, 8, 4, 2, 1)

VMEM_LIMIT = 56 * 1024 * 1024


def _cparams(sem):
    return pltpu.CompilerParams(dimension_semantics=sem, vmem_limit_bytes=VMEM_LIMIT)


def _dot(a, b):
    return jnp.dot(a, b, preferred_element_type=F32)


def _dot_nt(a, b):
    return lax.dot_general(a, b, (((1,), (1,)), ((), ())), preferred_element_type=F32)


def _dot_tn(a, b):
    return lax.dot_general(a, b, (((0,), (0,)), ((), ())), preferred_element_type=F32)


def _with_ones(v):
    return jnp.concatenate([v, jnp.ones(v.shape, v.dtype)], axis=1)


def _mod_kernel(a_ref, w_ref, b_ref, o_ref):
    a = a_ref[...]
    act = a / (1.0 + jnp.exp(-a))
    o_ref[0] = _dot(act.astype(BF16), w_ref[0].astype(BF16)) + b_ref[0]


def _modulation(cond_rows, w_mod, b_mod):
    depth, d, n = w_mod.shape
    tn = 1024
    return pl.pallas_call(
        _mod_kernel,
        grid=(depth, n // tn),
        in_specs=[pl.BlockSpec((8, d), lambda l, j: (0, 0)),
                  pl.BlockSpec((1, d, tn), lambda l, j: (l, 0, j)),
                  pl.BlockSpec((1, 1, tn), lambda l, j: (l, 0, j))],
        out_specs=pl.BlockSpec((1, 8, tn), lambda l, j: (l, 0, j)),
        out_shape=jax.ShapeDtypeStruct((depth, 8, n), F32),
        compiler_params=_cparams(("parallel", "parallel")),
        name="modulation",
    )(cond_rows, w_mod, b_mod.reshape(depth, 1, n))


def _rope_swap(a):
    lane = lax.broadcasted_iota(jnp.int32, a.shape, 1)
    first = (lane % 64) < 32
    return jnp.where(first, pltpu.roll(a, 96, 1), pltpu.roll(a, 32, 1))


def _inproj_kernel(x_ref, mod_ref, g_ref, w_ref, cos_ref, sin_ref, o_ref, *, rope, chunk):
    x = x_ref[...]
    ms = jnp.mean(x * x, axis=-1, keepdims=True)
    y = x * lax.rsqrt(ms + EPS) * g_ref[...]
    h = (y * (1.0 + mod_ref[0, 1:2, :]) + mod_ref[0, 0:1, :]).astype(BF16)
    for j0 in range(0, N_ZB, chunk):
        nblk = min(chunk, N_ZB - j0)
        acc = _dot(h, w_ref[:, j0 * LANES:(j0 + nblk) * LANES])
        for jj in range(nblk):
            j = j0 + jj
            a = acc[:, jj * LANES:(jj + 1) * LANES]
            if j in Q_SCALE:
                a = a * Q_SCALE[j]
            if rope and j in ROPE_BLOCKS:
                a = a * cos_ref[...] + _rope_swap(a) * sin_ref[...]
            o_ref[j] = a.astype(BF16)


def _inproj(xt, mod, gain, w, cos, sin, *, tm, tiles_per_batch, ctx_row):
    t, d = xt.shape
    rope = ctx_row is None
    if rope:
        mod_map = lambda i: (i // tiles_per_batch, 0, 0)
        pos_map = lambda i: (i % tiles_per_batch, 0)
    else:
        mod_map = lambda i: (ctx_row, 0, 0)
        pos_map = lambda i: (0, 0)
    return pl.pallas_call(
        functools.partial(_inproj_kernel, rope=rope, chunk=4),
        grid=(t // tm,),
        in_specs=[pl.BlockSpec((tm, d), lambda i: (i, 0)),
                  pl.BlockSpec((1, 6, d), mod_map),
                  pl.BlockSpec((1, d), lambda i: (0, 0)),
                  pl.BlockSpec((d, N_ZB * LANES), lambda i: (0, 0), pipeline_mode=pl.Buffered(1)),
                  pl.BlockSpec((tm, LANES), pos_map),
                  pl.BlockSpec((tm, LANES), pos_map)],
        out_specs=pl.BlockSpec((N_ZB, tm, LANES), lambda i: (0, i, 0)),
        out_shape=jax.ShapeDtypeStruct((N_ZB, t, LANES), BF16),
        compiler_params=_cparams(("parallel",)),
        name="in_proj",
    )(xt, mod, gain.reshape(1, d), w, cos, sin)


def _na_kernel(q_ref, k_ref, v_ref, kc_ref, vc_ref, bias_ref, o_ref, *, n_groups):
    gq = NA_GROUP_ROWS * GRID_W
    gk = NA_KEY_ROWS * GRID_W
    max_base = n_groups * NA_GROUP_ROWS - NA_KEY_ROWS
    kc = kc_ref[0]
    vc = _with_ones(vc_ref[0])

    def body(g, carry):
        base = jnp.clip(g * NA_GROUP_ROWS - NA_KH // 2, 0, max_base)
        kstart = pl.multiple_of(base * GRID_W, GRID_W)
        qstart = pl.multiple_of(g * gq, gq)
        q = q_ref[0, pl.ds(qstart, gq), :]
        k = k_ref[0, pl.ds(kstart, gk), :]
        v = _with_ones(v_ref[0, pl.ds(kstart, gk), :])
        sel = jnp.where(g == 0, 0, jnp.where(g == n_groups - 1, 2, 1))
        s_loc = _dot_nt(q, k) + bias_ref[0, sel]
        s_ctx = _dot_nt(q, kc)
        m = jnp.maximum(jnp.max(s_loc, axis=-1, keepdims=True), jnp.max(s_ctx, axis=-1, keepdims=True))
        p_loc = jnp.exp2(s_loc - m).astype(BF16)
        p_ctx = jnp.exp2(s_ctx - m).astype(BF16)
        o = _dot(p_loc, v) + _dot(p_ctx, vc)
        o_ref[pl.ds(qstart, gq), :] = (o[:, :LANES] / o[:, LANES:]).astype(BF16)
        return carry

    lax.fori_loop(0, n_groups, body, 0, unroll=4)


def _na_bias_tables(rpb, rows):
    kh = NA_KH
    heads = rpb.shape[0]
    c = np.arange(GRID_W)[:, None]
    kc = np.arange(GRID_W)[None, :]
    cs = np.clip(c - NA_KW // 2, 0, GRID_W - NA_KW)
    col_ok = (kc >= cs) & (kc < cs + NA_KW)
    d = np.arange(2 * NA_KW - 1)[:, None, None]
    onehot = ((kc - c + NA_KW - 1)[None] == d) & col_ok[None]
    toep = jnp.einsum('hrd,dck->hrck', rpb.astype(F32), jnp.asarray(onehot, F32),
                      precision=lax.Precision.HIGHEST)
    toep = toep * LOG2E + jnp.asarray(np.where(col_ok, 0.0, NEG_INF), F32)
    dead = jnp.full((heads, GRID_W, GRID_W), NEG_INF, F32)
    tabs = []
    for r0, base in ((0, 0), (kh // 2, 0), (rows - NA_GROUP_ROWS, rows - NA_KEY_ROWS)):
        blk_rows = []
        for rq in range(NA_GROUP_ROWS):
            r = r0 + rq
            rs = min(max(r - kh // 2, 0), rows - kh)
            blks = []
            for ki in range(NA_KEY_ROWS):
                krow = base + ki
                blks.append(toep[:, krow - r + NA_KH - 1] if rs <= krow < rs + kh else dead)
            blk_rows.append(jnp.concatenate(blks, axis=-1))
        tabs.append(jnp.concatenate(blk_rows, axis=-2))
    return jnp.stack(tabs, axis=1)


def _neighbourhood(z, zc, bias, *, batch, seq, n_ctx):
    rows = seq // GRID_W
    n_groups = rows // NA_GROUP_ROWS
    gq = NA_GROUP_ROWS * GRID_W
    gk = NA_KEY_ROWS * GRID_W
    return pl.pallas_call(
        functools.partial(_na_kernel, n_groups=n_groups),
        grid=(batch, NA_HEADS),
        in_specs=[pl.BlockSpec((1, seq, LANES), lambda b, h: (ZB_QA + h, b, 0)),
                  pl.BlockSpec((1, seq, LANES), lambda b, h: (ZB_KA + h, b, 0)),
                  pl.BlockSpec((1, seq, LANES), lambda b, h: (ZB_VA + h, b, 0)),
                  pl.BlockSpec((1, n_ctx, LANES), lambda b, h: (ZB_KA + h, b, 0)),
                  pl.BlockSpec((1, n_ctx, LANES), lambda b, h: (ZB_VA + h, b, 0)),
                  pl.BlockSpec((1, 3, gq, gk), lambda b, h: (h, 0, 0, 0))],
        out_specs=pl.BlockSpec((seq, LANES), lambda b, h: (b, h)),
        out_shape=jax.ShapeDtypeStruct((batch * seq, NA_HEADS * LANES), BF16),
        compiler_params=_cparams(("parallel", "parallel")),
        name="neighbourhood_attn",
    )(z, z, z, zc, zc, bias)


def _swa_kernel(sink_ref, q_ref, k_ref, v_ref, kc_ref, vc_ref, mask_ref, o_ref, *, blocks_per_step, n_blocks):
    grp = pl.program_id(1)
    step = pl.program_id(2)
    band = 3 * SWA_BLOCK
    kc = kc_ref[0]
    vc = vc_ref[0]

    def body(j, carry):
        n = step * blocks_per_step + j
        start = pl.multiple_of(jnp.clip((n - 1) * SWA_BLOCK, 0, (n_blocks - 3) * SWA_BLOCK), SWA_BLOCK)
        qs = pl.multiple_of(j * SWA_BLOCK, SWA_BLOCK)
        q = q_ref[:, pl.ds(qs, SWA_BLOCK), :].reshape(SWA_GROUP * SWA_BLOCK, LANES)
        k = k_ref[0, pl.ds(start, band), :]
        v = v_ref[0, pl.ds(start, band), :]
        sel = jnp.where(n == 0, 0, jnp.where(n == n_blocks - 1, 2, 1))
        msk = mask_ref[sel]
        s_loc = _dot_nt(q, k)
        s_ctx = _dot_nt(q, kc)
        for hq in range(SWA_GROUP):
            sl = s_loc[hq * SWA_BLOCK:(hq + 1) * SWA_BLOCK] + msk
            sc = s_ctx[hq * SWA_BLOCK:(hq + 1) * SWA_BLOCK]
            snk = sink_ref[grp * SWA_GROUP + hq] * LOG2E
            m = jnp.maximum(jnp.maximum(jnp.max(sl, axis=-1, keepdims=True),
                                        jnp.max(sc, axis=-1, keepdims=True)), snk)
            p_l = jnp.exp2(sl - m)
            p_c = jnp.exp2(sc - m)
            l = (jnp.sum(p_l, axis=-1, keepdims=True) + jnp.sum(p_c, axis=-1, keepdims=True)
                 + jnp.exp2(snk - m))
            o = _dot(p_l.astype(BF16), v) + _dot(p_c.astype(BF16), vc)
            o_ref[pl.ds(qs, SWA_BLOCK), hq * LANES:(hq + 1) * LANES] = (o / l).astype(BF16)
        return carry

    lax.fori_loop(0, blocks_per_step, body, 0, unroll=4)


def _swa_masks():
    iq = np.arange(SWA_BLOCK)[:, None]
    ik = np.arange(3 * SWA_BLOCK)[None, :]
    tabs = [np.where(np.abs(ik - off - iq) <= SWA_WINDOW, 0.0, NEG_INF) for off in (0, SWA_BLOCK, 2 * SWA_BLOCK)]
    return jnp.asarray(np.stack(tabs), F32)


def _window_attention(z, zc, sink, *, batch, seq, n_ctx, tq):
    n_blocks = seq // SWA_BLOCK
    steps = seq // tq
    return pl.pallas_call(
        functools.partial(_swa_kernel, blocks_per_step=tq // SWA_BLOCK, n_blocks=n_blocks),
        grid=(batch, SWA_KV_HEADS, steps),
        in_specs=[pl.BlockSpec(memory_space=pltpu.SMEM),
                  pl.BlockSpec((SWA_GROUP, tq, LANES), lambda b, g, i: (ZB_QS // SWA_GROUP + g, b * steps + i, 0)),
                  pl.BlockSpec((1, seq, LANES), lambda b, g, i: (ZB_KS + g, b, 0)),
                  pl.BlockSpec((1, seq, LANES), lambda b, g, i: (ZB_VS + g, b, 0)),
                  pl.BlockSpec((1, n_ctx, LANES), lambda b, g, i: (ZB_KS + g, b, 0)),
                  pl.BlockSpec((1, n_ctx, LANES), lambda b, g, i: (ZB_VS + g, b, 0)),
                  pl.BlockSpec((3, SWA_BLOCK, 3 * SWA_BLOCK), lambda b, g, i: (0, 0, 0))],
        out_specs=pl.BlockSpec((tq, SWA_GROUP * LANES), lambda b, g, i: (b * steps + i, g)),
        out_shape=jax.ShapeDtypeStruct((batch * seq, SWA_Q_HEADS * LANES), BF16),
        compiler_params=_cparams(("parallel", "parallel", "parallel")),
        name="window_attn",
    )(sink, z, z, z, zc, zc, _swa_masks())


def _ctx_attn_kernel(sink_ref, q_ref, k_ref, v_ref, o_ref, *, use_sink):
    q = q_ref[0]
    s = _dot_nt(q, k_ref[0])
    m = jnp.max(s, axis=-1, keepdims=True)
    if use_sink:
        snk = sink_ref[pl.program_id(1)] * LOG2E
        m = jnp.maximum(m, snk)
    p = jnp.exp2(s - m)
    l = jnp.sum(p, axis=-1, keepdims=True)
    if use_sink:
        l = l + jnp.exp2(snk - m)
    o_ref[...] = (_dot(p.astype(BF16), v_ref[0]) / l).astype(BF16)


def _ctx_attention(zc, sink, *, batch, n_ctx, q_blk, k_blk, v_blk, heads, group, use_sink):
    return pl.pallas_call(
        functools.partial(_ctx_attn_kernel, use_sink=use_sink),
        grid=(batch, heads),
        in_specs=[pl.BlockSpec(memory_space=pltpu.SMEM),
                  pl.BlockSpec((1, n_ctx, LANES), lambda b, h: (q_blk + h, b, 0)),
                  pl.BlockSpec((1, n_ctx, LANES), lambda b, h: (k_blk + h // group, b, 0)),
                  pl.BlockSpec((1, n_ctx, LANES), lambda b, h: (v_blk + h // group, b, 0))],
        out_specs=pl.BlockSpec((n_ctx, LANES), lambda b, h: (b, h)),
        out_shape=jax.ShapeDtypeStruct((batch * n_ctx, heads * LANES), BF16),
        compiler_params=_cparams(("parallel", "parallel")),
        name="context_attn",
    )(sink, zc, zc, zc)


def _gla_matrices():
    c = GLA_CHUNK
    i = np.arange(c)[:, None]
    t = np.arange(c)[None, :]
    incl = (t <= i)
    after = (t > i)
    lvl, masks = [], [(i == t)]
    for s in GLA_LEVELS:
        mid = (i // (2 * s)) * (2 * s) + s
        lvl.append(((i >= mid) & (t > mid) & (t <= i)) | ((i < mid) & (t > i) & (t <= mid)))
        bi = np.arange(c)[:, None] // s
        bj = np.arange(c)[None, :] // s
        masks.append((bi % 2 == 1) & (bj == bi - 1))
    fwd_out = np.concatenate([incl] + lvl, axis=0).astype(np.float32)
    fwd_scan = np.concatenate([after, np.ones((8, c), bool)], axis=0).astype(np.float32)
    fwd_mask = np.stack(masks).astype(np.float32)

    def flip_rows(m):
        return m.reshape(-1, c, c)[:, ::-1, ::-1].reshape(m.shape)

    out_m = np.stack([fwd_out, flip_rows(fwd_out)])
    scan_m = np.stack([fwd_scan, np.concatenate([flip_rows(fwd_scan[:c]), fwd_scan[c:]], axis=0)])
    mask_m = np.stack([fwd_mask, fwd_mask[:, ::-1, ::-1]])
    mask_m = np.concatenate([mask_m] * GLA_HEADS, axis=3)
    tile2 = lambda m: np.concatenate([m, m], axis=-1)
    return (jnp.asarray(tile2(out_m), BF16), jnp.asarray(tile2(scan_m), BF16), jnp.asarray(mask_m, F32))


def _split2(g):
    g1 = g.astype(BF16)
    g2 = (g - g1.astype(F32)).astype(BF16)
    return jnp.concatenate([g1, g2], axis=0)


def _gates_kernel(u_ref, w_ref, b_ref, o_ref):
    u = u_ref[0]
    for d in range(2):
        pre = _dot(u, w_ref[d]) + b_ref[d]
        logsig = jnp.minimum(pre, 0.0) - jnp.log(1.0 + jnp.exp(-jnp.abs(pre)))
        o_ref[d] = logsig * (1.0 / GLA_GATE_NORM)


def _gla_gates(z, wg, bg, *, tm):
    t = z.shape[1]
    kw = GLA_HEADS * GLA_DK
    return pl.pallas_call(
        _gates_kernel,
        grid=(t // tm,),
        in_specs=[pl.BlockSpec((1, tm, LANES), lambda i: (ZB_GATE, i, 0)),
                  pl.BlockSpec((2, LANES, kw), lambda i: (0, 0, 0)),
                  pl.BlockSpec((2, 1, kw), lambda i: (0, 0, 0))],
        out_specs=pl.BlockSpec((2, tm, kw), lambda i: (0, i, 0)),
        out_shape=jax.ShapeDtypeStruct((2, t, kw), F32),
        compiler_params=_cparams(("parallel",)),
        name="gla_gates",
    )(z, wg, bg)


def _head_of_lane(shape):
    return lax.broadcasted_iota(jnp.int32, shape, 1) // GLA_DK


def _cat_blocks(ref, rows):
    return jnp.concatenate([ref[j, rows, :] for j in range(ref.shape[0])], axis=1)


def _gla_scan_kernel(kf_ref, vf_ref, gf_ref, kb_ref, vb_ref, gb_ref, m_ref, s0_ref,
                     sprev_f_ref, sprev_b_ref, sfin_ref, s_acc, *, chunks):
    c = GLA_CHUNK

    @pl.when(pl.program_id(1) == 0)
    def _():
        s_acc[...] = s0_ref[0]

    head = _head_of_lane((LANES, GLA_HEADS * GLA_DK))

    def advance(d, k_ref, v_ref, g_ref, sprev_ref, cc):
        rows = pl.ds(pl.multiple_of(cc * c, c), c)
        e = _dot(m_ref[d], _split2(g_ref[0, rows, :]))
        khat = (_cat_blocks(k_ref, rows).astype(F32) * jnp.exp(e[0:c])).astype(BF16)
        full = _dot_tn(_cat_blocks(v_ref, rows), khat)
        upd = full[0:LANES]
        for h in range(1, GLA_HEADS):
            upd = jnp.where(head == h, full[h * LANES:(h + 1) * LANES], upd)
        s_old = s_acc[d]
        sprev_ref[0, cc] = s_old.astype(BF16)
        s_acc[d] = jnp.exp(e[c:c + 1]) * s_old + upd

    def body(jj, carry):
        advance(0, kf_ref, vf_ref, gf_ref, sprev_f_ref, jj)
        advance(1, kb_ref, vb_ref, gb_ref, sprev_b_ref, chunks - 1 - jj)
        return carry

    lax.fori_loop(0, chunks, body, 0, unroll=4)
    sfin_ref[0] = s_acc[...]


def _gla_scan(z, gates, scan_m, s0, *, batch, seq, tq):
    nt = seq // tq
    chunks = tq // GLA_CHUNK
    nc = seq // GLA_CHUNK
    kw = GLA_HEADS * GLA_DK
    fwd = lambda b, i: b * nt + i
    bwd = lambda b, i: b * nt + nt - 1 - i
    return pl.pallas_call(
        functools.partial(_gla_scan_kernel, chunks=chunks),
        grid=(batch, nt),
        in_specs=[pl.BlockSpec((2, tq, LANES), lambda b, i: (ZB_KG // 2, fwd(b, i), 0)),
                  pl.BlockSpec((4, tq, LANES), lambda b, i: (ZB_VG // 4, fwd(b, i), 0)),
                  pl.BlockSpec((1, tq, kw), lambda b, i: (0, fwd(b, i), 0)),
                  pl.BlockSpec((2, tq, LANES), lambda b, i: (ZB_KG // 2, bwd(b, i), 0)),
                  pl.BlockSpec((4, tq, LANES), lambda b, i: (ZB_VG // 4, bwd(b, i), 0)),
                  pl.BlockSpec((1, tq, kw), lambda b, i: (1, bwd(b, i), 0)),
                  pl.BlockSpec(scan_m.shape, lambda b, i: (0, 0, 0)),
                  pl.BlockSpec((1, 2, LANES, kw), lambda b, i: (b, 0, 0, 0))],
        out_specs=[pl.BlockSpec((1, chunks, LANES, kw), lambda b, i: (b, i, 0, 0)),
                   pl.BlockSpec((1, chunks, LANES, kw), lambda b, i: (b, nt - 1 - i, 0, 0)),
                   pl.BlockSpec((1, 2, LANES, kw), lambda b, i: (b, 0, 0, 0))],
        out_shape=[jax.ShapeDtypeStruct((batch, nc, LANES, kw), BF16),
                   jax.ShapeDtypeStruct((batch, nc, LANES, kw), BF16),
                   jax.ShapeDtypeStruct((batch, 2, LANES, kw), F32)],
        scratch_shapes=[pltpu.VMEM((2, LANES, kw), F32)],
        compiler_params=_cparams(("parallel", "arbitrary")),
        name="gla_scan",
    )(z, z, gates, z, z, gates, scan_m, s0)


def _gla_out_kernel(q_ref, k_ref, v_ref, r_ref, g_ref, sf_ref, sb_ref, m_ref, mask_ref, gain_ref, o_ref, *, chunks):
    c = GLA_CHUNK
    head = _head_of_lane((c, GLA_HEADS * GLA_DK))
    head_mask = [jnp.where(head == h, 1.0, 0.0).astype(BF16) for h in range(GLA_HEADS)]
    zero_blk = jnp.zeros((c, LANES), BF16)

    def stack_heads(ab):
        return jnp.concatenate([ab * head_mask[h] for h in range(GLA_HEADS)], axis=0)

    def body(cc, carry):
        rows = pl.ds(pl.multiple_of(cc * c, c), c)
        qb = _cat_blocks(q_ref, rows)
        kb = _cat_blocks(k_ref, rows)
        q = qb.astype(F32)
        k = kb.astype(F32)
        v_diag = jnp.concatenate(
            [jnp.concatenate([v_ref[h, rows, :] if j == h else zero_blk for j in range(GLA_HEADS)], axis=1)
             for h in range(GLA_HEADS)], axis=0)
        a = (2.0 * mask_ref[0, 0]) * _dot_nt(qb, stack_heads(kb))
        inters = []
        for d, s_ref in enumerate((sf_ref, sb_ref)):
            e = _dot(m_ref[d], _split2(g_ref[d, rows, :]))
            qhat = (q * jnp.exp(e[0:c])).astype(BF16)
            inters.append(_dot_nt(stack_heads(qhat), s_ref[0, cc]))
            for idx in range(len(GLA_LEVELS)):
                x = jnp.exp(e[(1 + idx) * c:(2 + idx) * c])
                a = a + mask_ref[d, 1 + idx] * _dot_nt((q * x).astype(BF16), stack_heads((k * x).astype(BF16)))
        o = _dot(a.astype(BF16), v_diag)
        for h in range(GLA_HEADS):
            oh = o[:, h * LANES:(h + 1) * LANES] + inters[0][h * c:(h + 1) * c] + inters[1][h * c:(h + 1) * c]
            of = oh * lax.rsqrt(jnp.mean(oh * oh, axis=-1, keepdims=True) + EPS) * gain_ref[...]
            r = r_ref[h, rows, :].astype(F32)
            o_ref[rows, h * LANES:(h + 1) * LANES] = (of * (r / (1.0 + jnp.exp(-r)))).astype(BF16)
        return carry

    lax.fori_loop(0, chunks, body, 0, unroll=2)


def _gla_output(z, gates, sprev_f, sprev_b, out_m, mask_m, gain, *, batch, seq, tq):
    nt = seq // tq
    chunks = tq // GLA_CHUNK
    kw = GLA_HEADS * GLA_DK
    tile = lambda b, i: (b * nt + i)
    return pl.pallas_call(
        functools.partial(_gla_out_kernel, chunks=chunks),
        grid=(batch, nt),
        in_specs=[pl.BlockSpec((2, tq, LANES), lambda b, i: (ZB_QG // 2, tile(b, i), 0)),
                  pl.BlockSpec((2, tq, LANES), lambda b, i: (ZB_KG // 2, tile(b, i), 0)),
                  pl.BlockSpec((4, tq, LANES), lambda b, i: (ZB_VG // 4, tile(b, i), 0)),
                  pl.BlockSpec((4, tq, LANES), lambda b, i: (ZB_RG // 4, tile(b, i), 0)),
                  pl.BlockSpec((2, tq, kw), lambda b, i: (0, tile(b, i), 0)),
                  pl.BlockSpec((1, chunks, LANES, kw), lambda b, i: (b, i, 0, 0)),
                  pl.BlockSpec((1, chunks, LANES, kw), lambda b, i: (b, i, 0, 0)),
                  pl.BlockSpec(out_m.shape, lambda b, i: (0, 0, 0)),
                  pl.BlockSpec(mask_m.shape, lambda b, i: (0, 0, 0, 0)),
                  pl.BlockSpec((1, LANES), lambda b, i: (0, 0))],
        out_specs=pl.BlockSpec((tq, GLA_HEADS * LANES), lambda b, i: (tile(b, i), 0)),
        out_shape=jax.ShapeDtypeStruct((batch * seq, GLA_HEADS * LANES), BF16),
        compiler_params=_cparams(("parallel", "parallel")),
        name="gla_output",
    )(z, z, z, z, gates, sprev_f, sprev_b, out_m, mask_m, gain.reshape(1, LANES))


def _outproj_kernel(ya_ref, yb_ref, ys_ref, w_ref, x_ref, mod_ref, g_ref, x1_ref, h2_ref):
    na = ya_ref.shape[1]
    nb = yb_ref.shape[1]
    proj = (_dot(ya_ref[...], w_ref[0:na, :]) + _dot(yb_ref[...], w_ref[na:na + nb, :])
            + _dot(ys_ref[...], w_ref[na + nb:, :]))
    x1 = x_ref[...] + mod_ref[0, 2:3, :] * proj
    x1_ref[...] = x1
    ms = jnp.mean(x1 * x1, axis=-1, keepdims=True)
    y = x1 * lax.rsqrt(ms + EPS) * g_ref[...]
    h2_ref[...] = (y * (1.0 + mod_ref[0, 4:5, :]) + mod_ref[0, 3:4, :]).astype(BF16)


def _outproj(ya, yb, ys, w_out, xt, mod, gain, *, tm, tiles_per_batch, ctx_row):
    t, d = xt.shape
    mod_map = (lambda i: (i // tiles_per_batch, 0, 0)) if ctx_row is None else (lambda i: (ctx_row, 0, 0))
    row = lambda i: (i, 0)
    return pl.pallas_call(
        _outproj_kernel,
        grid=(t // tm,),
        in_specs=[pl.BlockSpec((tm, ya.shape[1]), row),
                  pl.BlockSpec((tm, yb.shape[1]), row),
                  pl.BlockSpec((tm, ys.shape[1]), row),
                  pl.BlockSpec(w_out.shape, lambda i: (0, 0), pipeline_mode=pl.Buffered(1)),
                  pl.BlockSpec((tm, d), row),
                  pl.BlockSpec((1, 6, d), mod_map),
                  pl.BlockSpec((1, d), lambda i: (0, 0))],
        out_specs=[pl.BlockSpec((tm, d), row), pl.BlockSpec((tm, d), row)],
        out_shape=[jax.ShapeDtypeStruct((t, d), F32), jax.ShapeDtypeStruct((t, d), BF16)],
        compiler_params=_cparams(("parallel",)),
        name="out_proj",
    )(ya, yb, ys, w_out, xt, mod, gain.reshape(1, d))


def _mlp_kernel(h_ref, w1_ref, w2_ref, x_ref, mod_ref, g_ref, o_ref, acc_ref, *, final_norm):
    f = pl.program_id(1)

    @pl.when(f == 0)
    def _():
        acc_ref[...] = jnp.zeros_like(acc_ref)

    a = jnp.maximum(_dot(h_ref[...], w1_ref[...]), 0.0)
    acc_ref[...] += _dot((a * a).astype(BF16), w2_ref[...])

    @pl.when(f == pl.num_programs(1) - 1)
    def _():
        out = x_ref[...] + mod_ref[0, 5:6, :] * acc_ref[...]
        if final_norm:
            ms = jnp.mean(out * out, axis=-1, keepdims=True)
            out = out * lax.rsqrt(ms + EPS) * g_ref[...]
        o_ref[...] = out


def _mlp(h2, w1, w2, x1, mod, final_gain, *, tm, tf, tiles_per_batch, ctx_row, final_norm):
    t, d = x1.shape
    ff = w1.shape[1]
    mod_map = (lambda i, f: (i // tiles_per_batch, 0, 0)) if ctx_row is None else (lambda i, f: (ctx_row, 0, 0))
    return pl.pallas_call(
        functools.partial(_mlp_kernel, final_norm=final_norm),
        grid=(t // tm, ff // tf),
        in_specs=[pl.BlockSpec((tm, d), lambda i, f: (i, 0)),
                  pl.BlockSpec((d, tf), lambda i, f: (0, f)),
                  pl.BlockSpec((tf, d), lambda i, f: (f, 0)),
                  pl.BlockSpec((tm, d), lambda i, f: (i, 0)),
                  pl.BlockSpec((1, 6, d), mod_map),
                  pl.BlockSpec((1, d), lambda i, f: (0, 0))],
        out_specs=pl.BlockSpec((tm, d), lambda i, f: (i, 0)),
        out_shape=jax.ShapeDtypeStruct((t, d), F32),
        scratch_shapes=[pltpu.VMEM((tm, d), F32)],
        compiler_params=_cparams(("parallel", "arbitrary")),
        name="mlp",
    )(h2, w1, w2, x1, mod, final_gain.reshape(1, d))


def _rope_tables(seq):
    rows = seq // GRID_W
    n_freq = HEAD_DIM // 4
    inv = np.float32(ROPE_THETA) ** (-np.arange(n_freq, dtype=np.float32) / np.float32(n_freq))
    ar = (np.arange(rows, dtype=np.float32)[:, None] * inv[None]).astype(np.float64)
    ac = (np.arange(GRID_W, dtype=np.float32)[:, None] * inv[None]).astype(np.float64)
    cos_r = np.concatenate([np.cos(ar), np.cos(ar)], axis=-1).astype(np.float32)
    sin_r = np.concatenate([-np.sin(ar), np.sin(ar)], axis=-1).astype(np.float32)
    cos_c = np.concatenate([np.cos(ac), np.cos(ac)], axis=-1).astype(np.float32)
    sin_c = np.concatenate([-np.sin(ac), np.sin(ac)], axis=-1).astype(np.float32)

    def table(by_row, by_col):
        r = jnp.broadcast_to(jnp.asarray(by_row)[:, None, :], (rows, GRID_W, 2 * n_freq))
        c = jnp.broadcast_to(jnp.asarray(by_col)[None, :, :], (rows, GRID_W, 2 * n_freq))
        return jnp.concatenate([r, c], axis=-1).reshape(seq, 4 * n_freq)

    return table(cos_r, cos_c), table(sin_r, sin_c)


def _reorder_w_in(w_in):
    d = w_in.shape[0]
    g0 = (ZB_QS - 0) * LANES
    g1 = g0 + 2 * GLA_GATE_RANK
    pad = jnp.zeros((d, LANES - 2 * GLA_GATE_RANK), w_in.dtype)
    return jnp.concatenate([w_in[:, :g0], w_in[:, g1:], w_in[:, g0:g1], pad], axis=1).astype(BF16)


def _gate_weights(wg_f, bg_f, wg_b, bg_b):
    r = GLA_GATE_RANK
    kw = wg_f.shape[1]
    w = jnp.zeros((2, LANES, kw), F32)
    w = w.at[0, 0:r, :].set(wg_f)
    w = w.at[1, r:2 * r, :].set(wg_b)
    b = jnp.stack([bg_f.reshape(1, kw), bg_b.reshape(1, kw)])
    return w.astype(BF16), b.astype(F32)


@jax.jit
def _forward(x, c, ctx, c_ctx, w_mod, b_mod, norm1_g, norm2_g, w_in, na_rpb, gla_wg_fwd, gla_bg_fwd,
             gla_wg_bwd, gla_bg_bwd, gla_norm_g, swa_sink, w_out, w_ff1, w_ff2, final_norm_g):
    batch, seq, d = x.shape
    n_ctx = ctx.shape[1]
    depth = w_mod.shape[0]
    ctx_row = batch
    tm = 512
    tpb = seq // tm

    cond = jnp.concatenate([c, c_ctx[None], jnp.zeros((8 - batch - 1, d), F32)], axis=0)
    mods = _modulation(cond, w_mod, b_mod).reshape(depth, 8, 6, d)
    cos, sin = _rope_tables(seq)
    out_m, scan_m, mask_m = _gla_matrices()
    zero_state = jnp.zeros((batch, 2, LANES, GLA_HEADS * GLA_DK), F32)

    xl = x.reshape(batch * seq, d)
    xc = ctx.reshape(batch * n_ctx, d)
    for i in range(depth):
        with_ctx = i < depth - 1
        mod = mods[i]
        w_in_r = _reorder_w_in(w_in[i])
        w_out_b = w_out[i].astype(BF16)
        w1 = w_ff1[i].astype(BF16)
        w2 = w_ff2[i].astype(BF16)
        wg, bg = _gate_weights(gla_wg_fwd[i], gla_bg_fwd[i], gla_wg_bwd[i], gla_bg_bwd[i])
        bias = _na_bias_tables(na_rpb[i], seq // GRID_W)

        z = _inproj(xl, mod, norm1_g[i], w_in_r, cos, sin, tm=tm, tiles_per_batch=tpb, ctx_row=None)
        zc = _inproj(xc, mod, norm1_g[i], w_in_r, cos, sin, tm=n_ctx, tiles_per_batch=1, ctx_row=ctx_row)

        gates_c = _gla_gates(zc, wg, bg, tm=n_ctx)
        gates_l = _gla_gates(z, wg, bg, tm=min(2048, seq))
        spf_c, spb_c, sfin_c = _gla_scan(zc, gates_c, scan_m, zero_state, batch=batch, seq=n_ctx, tq=n_ctx)
        spf_l, spb_l, _ = _gla_scan(z, gates_l, scan_m, sfin_c, batch=batch, seq=seq, tq=min(2048, seq))

        ya = _neighbourhood(z, zc, bias, batch=batch, seq=seq, n_ctx=n_ctx)
        yb = _gla_output(z, gates_l, spf_l, spb_l, out_m, mask_m, gla_norm_g[i], batch=batch, seq=seq, tq=512)
        ys = _window_attention(z, zc, swa_sink[i], batch=batch, seq=seq, n_ctx=n_ctx, tq=min(1024, seq))

        x1, h2 = _outproj(ya, yb, ys, w_out_b, xl, mod, norm2_g[i], tm=tm, tiles_per_batch=tpb, ctx_row=None)
        xl = _mlp(h2, w1, w2, x1, mod, final_norm_g, tm=tm, tf=1024, tiles_per_batch=tpb, ctx_row=None,
                  final_norm=not with_ctx)

        if with_ctx:
            ya_c = _ctx_attention(zc, swa_sink[i], batch=batch, n_ctx=n_ctx, q_blk=ZB_QA, k_blk=ZB_KA,
                                  v_blk=ZB_VA, heads=NA_HEADS, group=1, use_sink=False)
            yb_c = _gla_output(zc, gates_c, spf_c, spb_c, out_m, mask_m, gla_norm_g[i], batch=batch, seq=n_ctx,
                               tq=n_ctx)
            ys_c = _ctx_attention(zc, swa_sink[i], batch=batch, n_ctx=n_ctx, q_blk=ZB_QS, k_blk=ZB_KS,
                                  v_blk=ZB_VS, heads=SWA_Q_HEADS, group=SWA_GROUP, use_sink=True)
            x1c, h2c = _outproj(ya_c, yb_c, ys_c, w_out_b, xc, mod, norm2_g[i], tm=n_ctx, tiles_per_batch=1,
                                ctx_row=ctx_row)
            xc = _mlp(h2c, w1, w2, x1c, mod, final_norm_g, tm=n_ctx, tf=1024, tiles_per_batch=1,
                      ctx_row=ctx_row, final_norm=False)
    return xl.reshape(batch, seq, d)


def kernel(x, c, ctx, c_ctx, w_mod, b_mod, norm1_g, norm2_g, w_in, na_rpb, gla_wg_fwd, gla_bg_fwd, gla_wg_bwd,
           gla_bg_bwd, gla_norm_g, swa_sink, w_out, w_ff1, w_ff2, final_norm_g):
    return _forward(x, c, ctx, c_ctx, w_mod, b_mod, norm1_g, norm2_g, w_in, na_rpb, gla_wg_fwd, gla_bg_fwd,
                    gla_wg_bwd, gla_bg_bwd, gla_norm_g, swa_sink, w_out, w_ff1, w_ff2, final_norm_g)
```

```python
import functools

import numpy as np
import jax
import jax.numpy as jnp
from jax import lax
from jax.experimental import pallas as pl
from jax.experimental.pallas import tpu as pltpu

F32 = jnp.float32
BF16 = jnp.bfloat16

GRID_W = 64
HEAD_DIM = 128
LANES = 128
NA_HEADS = 4
NA_KH = 8
NA_KW = 16
GLA_HEADS = 4
GLA_DK = 64
GLA_GATE_RANK = 16
GLA_GATE_NORM = 16.0
GLA_CHUNK = 64
GLA_PAIRS = GLA_HEADS // 2
SWA_Q_HEADS = 8
SWA_KV_HEADS = 2
SWA_GROUP = SWA_Q_HEADS // SWA_KV_HEADS
SWA_WINDOW = 128
SWA_BLOCK = 128
ROPE_THETA = 10000.0
EPS = 1e-6
NEG_INF = -1e30

ZB_QA, ZB_KA, ZB_VA = 0, 4, 8
ZB_QG, ZB_KG, ZB_VG, ZB_RG = 12, 14, 16, 20
ZB_QS, ZB_KS, ZB_VS = 24, 32, 34
ZB_GATE = 36
N_ZB = 37
ROPE_BLOCKS = tuple(range(ZB_QS, ZB_VS))
ATTN_KEY_BLOCKS = tuple(range(ZB_KS, ZB_VS))
LOG2E = 1.4426950408889634
ATTN_SCALE = HEAD_DIM ** -0.5 * LOG2E
Q_SCALE = {**{j: ATTN_SCALE for j in range(ZB_QA, ZB_KA)},
           **{j: GLA_DK ** -0.5 for j in range(ZB_QG, ZB_KG)},
           **{j: ATTN_SCALE for j in range(ZB_QS, ZB_KS)}}

NA_GROUP_ROWS = 4
NA_KEY_ROWS = 12
GLA_LEVELS = (32, 16, 8, 4, 2, 1)

VMEM_LIMIT = 56 * 1024 * 1024


def _cparams(sem):
    return pltpu.CompilerParams(dimension_semantics=sem, vmem_limit_bytes=VMEM_LIMIT)


def _dot(a, b):
    return jnp.dot(a, b, preferred_element_type=F32)


def _dot_nt(a, b):
    return lax.dot_general(a, b, (((1,), (1,)), ((), ())), preferred_element_type=F32)


def _dot_tn(a, b):
    return lax.dot_general(a, b, (((0,), (0,)), ((), ())), preferred_element_type=F32)


def _with_ones(v):
    return jnp.concatenate([v, jnp.ones(v.shape, v.dtype)], axis=1)


def _mod_kernel(a_ref, w_ref, b_ref, o_ref):
    a = a_ref[...]
    act = a / (1.0 + jnp.exp(-a))
    o_ref[0] = _dot(act.astype(BF16), w_ref[0].astype(BF16)) + b_ref[0]


def _modulation(cond_rows, w_mod, b_mod):
    depth, d, n = w_mod.shape
    tn = 1024
    return pl.pallas_call(
        _mod_kernel,
        grid=(depth, n // tn),
        in_specs=[pl.BlockSpec((8, d), lambda l, j: (0, 0)),
                  pl.BlockSpec((1, d, tn), lambda l, j: (l, 0, j)),
                  pl.BlockSpec((1, 1, tn), lambda l, j: (l, 0, j))],
        out_specs=pl.BlockSpec((1, 8, tn), lambda l, j: (l, 0, j)),
        out_shape=jax.ShapeDtypeStruct((depth, 8, n), F32),
        compiler_params=_cparams(("parallel", "parallel")),
        name="modulation",
    )(cond_rows, w_mod, b_mod.reshape(depth, 1, n))


def _rope_swap(a):
    lane = lax.broadcasted_iota(jnp.int32, a.shape, 1)
    first = (lane % 64) < 32
    return jnp.where(first, pltpu.roll(a, 96, 1), pltpu.roll(a, 32, 1))


def _log_gates(u, wg_ref, bg_ref, out_ref, rs):
    for d in range(2):
        pre = _dot(u, wg_ref[d]) + bg_ref[d]
        logsig = jnp.minimum(pre, 0.0) - jnp.log(1.0 + jnp.exp(-jnp.abs(pre)))
        out_ref[d, rs, :] = logsig * (1.0 / GLA_GATE_NORM)


def _inproj_kernel(x_ref, mod_ref, g_ref, wa_ref, wb_ref, wr_ref, cos_ref, sin_ref, wg_ref, bg_ref,
                   o_ref, gates_ref, kt_ref, *, rope, chunk, n_split):
    segments = ((0, wa_ref), (ZB_QS, wb_ref), (ZB_GATE, wr_ref))
    rows = x_ref.shape[0] // n_split
    for r in range(n_split):
        rs = slice(r * rows, (r + 1) * rows)
        x = x_ref[rs, :]
        ms = jnp.mean(x * x, axis=-1, keepdims=True)
        y = x * lax.rsqrt(ms + EPS) * g_ref[...]
        h = (y * (1.0 + mod_ref[0, 1:2, :]) + mod_ref[0, 0:1, :]).astype(BF16)
        starts = [(first + c0, w_ref, c0) for first, w_ref in segments
                  for c0 in range(0, w_ref.shape[1] // LANES, chunk)]
        for j0, w_ref, c0 in starts:
            nblk = min(chunk, w_ref.shape[1] // LANES - c0)
            acc = _dot(h, w_ref[:, c0 * LANES:(c0 + nblk) * LANES])
            for jj in range(nblk):
                j = j0 + jj
                a = acc[:, jj * LANES:(jj + 1) * LANES]
                if j in Q_SCALE:
                    a = a * Q_SCALE[j]
                if rope and j in ROPE_BLOCKS:
                    a = a * cos_ref[rs, :] + _rope_swap(a) * sin_ref[rs, :]
                if j == ZB_GATE:
                    _log_gates(a.astype(BF16), wg_ref, bg_ref, gates_ref, rs)
                else:
                    o_ref[j, rs, :] = a.astype(BF16)
                if j in ATTN_KEY_BLOCKS:
                    kt_ref[ATTN_KEY_BLOCKS.index(j), :, rs] = a.T.astype(BF16)


def _inproj(xt, mod, gain, w, cos, sin, wg, bg, *, tm, tiles_per_batch, ctx_row):
    t, d = xt.shape
    kw = wg.shape[-1]
    rope = ctx_row is None
    if rope:
        mod_map = lambda i: (i // tiles_per_batch, 0, 0)
        pos_map = lambda i: (i % tiles_per_batch, 0)
    else:
        mod_map = lambda i: (ctx_row, 0, 0)
        pos_map = lambda i: (0, 0)
    return pl.pallas_call(
        functools.partial(_inproj_kernel, rope=rope, chunk=4, n_split=2),
        grid=(t // tm,),
        in_specs=[pl.BlockSpec((tm, d), lambda i: (i, 0)),
                  pl.BlockSpec((1, 6, d), mod_map),
                  pl.BlockSpec((1, d), lambda i: (0, 0)),
                  *[pl.BlockSpec(wseg.shape, lambda i: (0, 0), pipeline_mode=pl.Buffered(1)) for wseg in w],
                  pl.BlockSpec((tm, LANES), pos_map),
                  pl.BlockSpec((tm, LANES), pos_map),
                  pl.BlockSpec((2, LANES, kw), lambda i: (0, 0, 0)),
                  pl.BlockSpec((2, 1, kw), lambda i: (0, 0, 0))],
        out_specs=[pl.BlockSpec((ZB_GATE, tm, LANES), lambda i: (0, i, 0)),
                   pl.BlockSpec((2, tm, kw), lambda i: (0, i, 0)),
                   pl.BlockSpec((len(ATTN_KEY_BLOCKS), LANES, tm), lambda i: (0, 0, i))],
        out_shape=[jax.ShapeDtypeStruct((ZB_GATE, t, LANES), BF16),
                   jax.ShapeDtypeStruct((2, t, kw), F32),
                   jax.ShapeDtypeStruct((len(ATTN_KEY_BLOCKS), LANES, t), BF16)],
        compiler_params=_cparams(("parallel",)),
        name="in_proj",
    )(xt, mod, gain.reshape(1, d), *w, cos, sin, wg, bg)


def _na_kernel(q_ref, k_ref, v_ref, kc_ref, vc_ref, toep_ref, o_ref, bias_ref, *, n_groups):
    gq = NA_GROUP_ROWS * GRID_W
    gk = NA_KEY_ROWS * GRID_W
    max_base = n_groups * NA_GROUP_ROWS - NA_KEY_ROWS
    _na_fill_bias(toep_ref, bias_ref, n_groups * NA_GROUP_ROWS)
    kc = kc_ref[0]
    vc = _with_ones(vc_ref[0])

    def body(g, carry):
        base = jnp.clip(g * NA_GROUP_ROWS - NA_KH // 2, 0, max_base)
        kstart = pl.multiple_of(base * GRID_W, NA_GROUP_ROWS * GRID_W)
        qstart = pl.multiple_of(g * gq, gq)
        q = q_ref[0, pl.ds(qstart, gq), :]
        k = k_ref[0, pl.ds(kstart, gk), :]
        v = _with_ones(v_ref[0, pl.ds(kstart, gk), :])
        sel = jnp.where(g == 0, 0, jnp.where(g == n_groups - 1, 2, 1))
        s_loc = _dot_nt(q, k) + bias_ref[sel]
        s_ctx = _dot_nt(q, kc)
        m = jnp.maximum(jnp.max(s_loc, axis=-1, keepdims=True), jnp.max(s_ctx, axis=-1, keepdims=True))
        p_loc = jnp.exp2(s_loc - m).astype(BF16)
        p_ctx = jnp.exp2(s_ctx - m).astype(BF16)
        o = _dot(p_loc, v) + _dot(p_ctx, vc)
        o_ref[pl.ds(qstart, gq), :] = (o[:, :LANES] / o[:, LANES:]).astype(BF16)
        return carry

    lax.fori_loop(0, n_groups, body, 0, unroll=4)


def _na_toeplitz(rpb):
    c = np.arange(GRID_W)[:, None]
    kc = np.arange(GRID_W)[None, :]
    cs = np.clip(c - NA_KW // 2, 0, GRID_W - NA_KW)
    col_ok = (kc >= cs) & (kc < cs + NA_KW)
    d = np.arange(2 * NA_KW - 1)[:, None, None]
    onehot = ((kc - c + NA_KW - 1)[None] == d) & col_ok[None]
    toep = jnp.einsum('hrd,dck->hrck', rpb.astype(F32), jnp.asarray(onehot, F32),
                      precision=lax.Precision.HIGHEST)
    toep = toep * LOG2E + jnp.asarray(np.where(col_ok, 0.0, NEG_INF), F32)
    return jnp.concatenate([toep, toep], axis=-1)


def _na_group_cases(rows):
    return ((0, 0), (NA_KH // 2, 0), (rows - NA_GROUP_ROWS, rows - NA_KEY_ROWS))


def _na_fill_bias(toep_ref, bias_ref, rows):
    lane = lax.broadcasted_iota(jnp.int32, (GRID_W, LANES), 1)
    dead = jnp.full((GRID_W, LANES), NEG_INF, F32)
    for case, (r0, base) in enumerate(_na_group_cases(rows)):
        for rq in range(NA_GROUP_ROWS):
            r = r0 + rq
            rs = min(max(r - NA_KH // 2, 0), rows - NA_KH)
            for kp in range(NA_KEY_ROWS // 2):
                halves = []
                for krow in (base + 2 * kp, base + 2 * kp + 1):
                    halves.append(toep_ref[0, krow - r + NA_KH - 1] if rs <= krow < rs + NA_KH else dead)
                bias_ref[case, rq * GRID_W:(rq + 1) * GRID_W, kp * LANES:(kp + 1) * LANES] = (
                    jnp.where(lane < GRID_W, halves[0], halves[1]))


def _neighbourhood(z, zc, toep, *, batch, seq, n_ctx):
    rows = seq // GRID_W
    n_groups = rows // NA_GROUP_ROWS
    gq = NA_GROUP_ROWS * GRID_W
    gk = NA_KEY_ROWS * GRID_W
    return pl.pallas_call(
        functools.partial(_na_kernel, n_groups=n_groups),
        grid=(batch, NA_HEADS),
        in_specs=[pl.BlockSpec((1, seq, LANES), lambda b, h: (ZB_QA + h, b, 0)),
                  pl.BlockSpec((1, seq, LANES), lambda b, h: (ZB_KA + h, b, 0)),
                  pl.BlockSpec((1, seq, LANES), lambda b, h: (ZB_VA + h, b, 0)),
                  pl.BlockSpec((1, n_ctx, LANES), lambda b, h: (ZB_KA + h, b, 0)),
                  pl.BlockSpec((1, n_ctx, LANES), lambda b, h: (ZB_VA + h, b, 0)),
                  pl.BlockSpec((1,) + toep.shape[1:], lambda b, h: (h, 0, 0, 0))],
        out_specs=pl.BlockSpec((seq, LANES), lambda b, h: (b, h)),
        out_shape=jax.ShapeDtypeStruct((batch * seq, NA_HEADS * LANES), BF16),
        scratch_shapes=[pltpu.VMEM((3, gq, gk), F32)],
        compiler_params=_cparams(("parallel", "parallel")),
        name="neighbourhood_attn",
    )(z, z, z, zc, zc, toep)


def _swa_kernel(sink_ref, q_ref, k_ref, v_ref, kc_ref, vc_ref, mask_ref, o_ref, *, blocks_per_step, n_blocks):
    grp = pl.program_id(1)
    step = pl.program_id(2)
    band = 3 * SWA_BLOCK
    kc = kc_ref[0]
    vc = vc_ref[0]

    def body(j, carry):
        n = step * blocks_per_step + j
        start = pl.multiple_of(jnp.clip((n - 1) * SWA_BLOCK, 0, (n_blocks - 3) * SWA_BLOCK), SWA_BLOCK)
        qs = pl.multiple_of(j * SWA_BLOCK, SWA_BLOCK)
        q = q_ref[:, pl.ds(qs, SWA_BLOCK), :].reshape(SWA_GROUP * SWA_BLOCK, LANES)
        kt = k_ref[0, :, pl.ds(start, band)]
        v = v_ref[0, pl.ds(start, band), :]
        sel = jnp.where(n == 0, 0, jnp.where(n == n_blocks - 1, 2, 1))
        msk = mask_ref[sel]
        s_loc = _dot(q, kt)
        s_ctx = _dot(q, kc)
        for hq in range(SWA_GROUP):
            sl = s_loc[hq * SWA_BLOCK:(hq + 1) * SWA_BLOCK] + msk
            sc = s_ctx[hq * SWA_BLOCK:(hq + 1) * SWA_BLOCK]
            snk = sink_ref[grp * SWA_GROUP + hq] * LOG2E
            m = jnp.maximum(jnp.maximum(jnp.max(sl, axis=-1, keepdims=True),
                                        jnp.max(sc, axis=-1, keepdims=True)), snk)
            p_l = jnp.exp2(sl - m)
            p_c = jnp.exp2(sc - m)
            l = (jnp.sum(p_l, axis=-1, keepdims=True) + jnp.sum(p_c, axis=-1, keepdims=True)
                 + jnp.exp2(snk - m))
            o = _dot(p_l.astype(BF16), v) + _dot(p_c.astype(BF16), vc)
            o_ref[pl.ds(qs, SWA_BLOCK), hq * LANES:(hq + 1) * LANES] = (o / l).astype(BF16)
        return carry

    lax.fori_loop(0, blocks_per_step, body, 0, unroll=4)


def _swa_masks():
    iq = np.arange(SWA_BLOCK)[:, None]
    ik = np.arange(3 * SWA_BLOCK)[None, :]
    tabs = [np.where(np.abs(ik - off - iq) <= SWA_WINDOW, 0.0, NEG_INF) for off in (0, SWA_BLOCK, 2 * SWA_BLOCK)]
    return jnp.asarray(np.stack(tabs), F32)


def _window_attention(z, zc, kt, ktc, sink, *, batch, seq, n_ctx, tq):
    n_blocks = seq // SWA_BLOCK
    steps = seq // tq
    return pl.pallas_call(
        functools.partial(_swa_kernel, blocks_per_step=tq // SWA_BLOCK, n_blocks=n_blocks),
        grid=(batch, SWA_KV_HEADS, steps),
        in_specs=[pl.BlockSpec(memory_space=pltpu.SMEM),
                  pl.BlockSpec((SWA_GROUP, tq, LANES), lambda b, g, i: (ZB_QS // SWA_GROUP + g, b * steps + i, 0)),
                  pl.BlockSpec((1, LANES, seq), lambda b, g, i: (g, 0, b)),
                  pl.BlockSpec((1, seq, LANES), lambda b, g, i: (ZB_VS + g, b, 0)),
                  pl.BlockSpec((1, LANES, n_ctx), lambda b, g, i: (g, 0, b)),
                  pl.BlockSpec((1, n_ctx, LANES), lambda b, g, i: (ZB_VS + g, b, 0)),
                  pl.BlockSpec((3, SWA_BLOCK, 3 * SWA_BLOCK), lambda b, g, i: (0, 0, 0))],
        out_specs=pl.BlockSpec((tq, SWA_GROUP * LANES), lambda b, g, i: (b * steps + i, g)),
        out_shape=jax.ShapeDtypeStruct((batch * seq, SWA_Q_HEADS * LANES), BF16),
        compiler_params=_cparams(("parallel", "parallel", "parallel")),
        name="window_attn",
    )(sink, z, kt, z, ktc, zc, _swa_masks())


def _ctx_attn_kernel(sink_ref, q_ref, k_ref, v_ref, o_ref, *, use_sink):
    q = q_ref[0]
    s = _dot_nt(q, k_ref[0])
    m = jnp.max(s, axis=-1, keepdims=True)
    if use_sink:
        snk = sink_ref[pl.program_id(1)] * LOG2E
        m = jnp.maximum(m, snk)
    p = jnp.exp2(s - m)
    l = jnp.sum(p, axis=-1, keepdims=True)
    if use_sink:
        l = l + jnp.exp2(snk - m)
    o_ref[...] = (_dot(p.astype(BF16), v_ref[0]) / l).astype(BF16)


def _ctx_attention(zc, sink, *, batch, n_ctx, q_blk, k_blk, v_blk, heads, group, use_sink):
    return pl.pallas_call(
        functools.partial(_ctx_attn_kernel, use_sink=use_sink),
        grid=(batch, heads),
        in_specs=[pl.BlockSpec(memory_space=pltpu.SMEM),
                  pl.BlockSpec((1, n_ctx, LANES), lambda b, h: (q_blk + h, b, 0)),
                  pl.BlockSpec((1, n_ctx, LANES), lambda b, h: (k_blk + h // group, b, 0)),
                  pl.BlockSpec((1, n_ctx, LANES), lambda b, h: (v_blk + h // group, b, 0))],
        out_specs=pl.BlockSpec((n_ctx, LANES), lambda b, h: (b, h)),
        out_shape=jax.ShapeDtypeStruct((batch * n_ctx, heads * LANES), BF16),
        compiler_params=_cparams(("parallel", "parallel")),
        name="context_attn",
    )(sink, zc, zc, zc)


def _gla_matrices():
    c = GLA_CHUNK
    i = np.arange(c)[:, None]
    t = np.arange(c)[None, :]
    incl = (t <= i)
    after = (t > i)
    lvl, masks = [], [(i == t)]
    for s in GLA_LEVELS:
        mid = (i // (2 * s)) * (2 * s) + s
        lvl.append(((i >= mid) & (t > mid) & (t <= i)) | ((i < mid) & (t > i) & (t <= mid)))
        bi = np.arange(c)[:, None] // s
        bj = np.arange(c)[None, :] // s
        masks.append((bi % 2 == 1) & (bj == bi - 1))
    fwd_out = np.concatenate([incl] + lvl, axis=0).astype(np.float32)
    fwd_scan = np.concatenate([after, np.ones((8, c), bool)], axis=0).astype(np.float32)
    fwd_mask = np.stack(masks).astype(np.float32)

    def flip_rows(m):
        return m.reshape(-1, c, c)[:, ::-1, ::-1].reshape(m.shape)

    out_m = np.stack([fwd_out, flip_rows(fwd_out)])
    scan_m = np.stack([fwd_scan, np.concatenate([flip_rows(fwd_scan[:c]), fwd_scan[c:]], axis=0)])
    mask_m = np.stack([fwd_mask, fwd_mask[:, ::-1, ::-1]])
    mask_m = np.concatenate([mask_m] * GLA_HEADS, axis=3)
    tile2 = lambda m: np.concatenate([m, m], axis=-1)
    return (jnp.asarray(tile2(out_m), BF16), jnp.asarray(tile2(scan_m), BF16), jnp.asarray(mask_m, F32))


def _split2(g):
    g1 = g.astype(BF16)
    g2 = (g - g1.astype(F32)).astype(BF16)
    return jnp.concatenate([g1, g2], axis=0)


def _head_of_lane(shape):
    return lax.broadcasted_iota(jnp.int32, shape, 1) // GLA_DK


def _cat_blocks(ref, rows):
    return jnp.concatenate([ref[j, rows, :] for j in range(ref.shape[0])], axis=1)


def _gla_scan_kernel(kf_ref, vf_ref, gf_ref, kb_ref, vb_ref, gb_ref, m_ref, s0_ref,
                     sprev_f_ref, sprev_b_ref, sfin_ref, s_acc, *, chunks):
    c = GLA_CHUNK

    @pl.when(pl.program_id(1) == 0)
    def _():
        s_acc[...] = s0_ref[0]

    head = _head_of_lane((LANES, GLA_HEADS * GLA_DK))

    def advance(d, k_ref, v_ref, g_ref, sprev_ref, cc):
        rows = pl.ds(pl.multiple_of(cc * c, c), c)
        e = _dot(m_ref[d], _split2(g_ref[0, rows, :]))
        khat = (_cat_blocks(k_ref, rows).astype(F32) * jnp.exp(e[0:c])).astype(BF16)
        full = _dot_tn(_cat_blocks(v_ref, rows), khat)
        upd = full[0:LANES]
        for h in range(1, GLA_HEADS):
            upd = jnp.where(head == h, full[h * LANES:(h + 1) * LANES], upd)
        s_old = s_acc[d]
        sprev_ref[0, cc] = s_old.astype(BF16)
        s_acc[d] = jnp.exp(e[c:c + 1]) * s_old + upd

    def body(jj, carry):
        advance(0, kf_ref, vf_ref, gf_ref, sprev_f_ref, jj)
        advance(1, kb_ref, vb_ref, gb_ref, sprev_b_ref, chunks - 1 - jj)
        return carry

    lax.fori_loop(0, chunks, body, 0, unroll=4)
    sfin_ref[0] = s_acc[...]


def _gla_scan(z, gates, scan_m, s0, *, batch, seq, tq):
    nt = seq // tq
    chunks = tq // GLA_CHUNK
    nc = seq // GLA_CHUNK
    kw = GLA_HEADS * GLA_DK
    fwd = lambda b, i: b * nt + i
    bwd = lambda b, i: b * nt + nt - 1 - i
    return pl.pallas_call(
        functools.partial(_gla_scan_kernel, chunks=chunks),
        grid=(batch, nt),
        in_specs=[pl.BlockSpec((2, tq, LANES), lambda b, i: (ZB_KG // 2, fwd(b, i), 0)),
                  pl.BlockSpec((4, tq, LANES), lambda b, i: (ZB_VG // 4, fwd(b, i), 0)),
                  pl.BlockSpec((1, tq, kw), lambda b, i: (0, fwd(b, i), 0)),
                  pl.BlockSpec((2, tq, LANES), lambda b, i: (ZB_KG // 2, bwd(b, i), 0)),
                  pl.BlockSpec((4, tq, LANES), lambda b, i: (ZB_VG // 4, bwd(b, i), 0)),
                  pl.BlockSpec((1, tq, kw), lambda b, i: (1, bwd(b, i), 0)),
                  pl.BlockSpec(scan_m.shape, lambda b, i: (0, 0, 0)),
                  pl.BlockSpec((1, 2, LANES, kw), lambda b, i: (b, 0, 0, 0))],
        out_specs=[pl.BlockSpec((1, chunks, LANES, kw), lambda b, i: (b, i, 0, 0)),
                   pl.BlockSpec((1, chunks, LANES, kw), lambda b, i: (b, nt - 1 - i, 0, 0)),
                   pl.BlockSpec((1, 2, LANES, kw), lambda b, i: (b, 0, 0, 0))],
        out_shape=[jax.ShapeDtypeStruct((batch, nc, LANES, kw), BF16),
                   jax.ShapeDtypeStruct((batch, nc, LANES, kw), BF16),
                   jax.ShapeDtypeStruct((batch, 2, LANES, kw), F32)],
        scratch_shapes=[pltpu.VMEM((2, LANES, kw), F32)],
        compiler_params=_cparams(("parallel", "arbitrary")),
        name="gla_scan",
    )(z, z, gates, z, z, gates, scan_m, s0)


def _gla_out_kernel(q_ref, k_ref, v_ref, r_ref, g_ref, sf_ref, sb_ref, m_ref, mask_ref, gain_ref, o_ref, *, chunks):
    c = GLA_CHUNK
    head = _head_of_lane((c, GLA_HEADS * GLA_DK))
    head_mask = [jnp.where(head == h, 1.0, 0.0).astype(BF16) for h in range(GLA_HEADS)]
    zero_blk = jnp.zeros((c, LANES), BF16)

    def stack_heads(ab):
        return jnp.concatenate([ab * head_mask[h] for h in range(GLA_HEADS)], axis=0)

    def body(cc, carry):
        rows = pl.ds(pl.multiple_of(cc * c, c), c)
        qb = _cat_blocks(q_ref, rows)
        kb = _cat_blocks(k_ref, rows)
        q = qb.astype(F32)
        k = kb.astype(F32)
        v_diag = jnp.concatenate(
            [jnp.concatenate([v_ref[h, rows, :] if j == h else zero_blk for j in range(GLA_HEADS)], axis=1)
             for h in range(GLA_HEADS)], axis=0)
        a = (2.0 * mask_ref[0, 0]) * _dot_nt(qb, stack_heads(kb))
        inters = []
        for d, s_ref in enumerate((sf_ref, sb_ref)):
            e = _dot(m_ref[d], _split2(g_ref[d, rows, :]))
            qhat = (q * jnp.exp(e[0:c])).astype(BF16)
            inters.append(_dot_nt(stack_heads(qhat), s_ref[0, cc]))
            for idx in range(len(GLA_LEVELS)):
                x = jnp.exp(e[(1 + idx) * c:(2 + idx) * c])
                a = a + mask_ref[d, 1 + idx] * _dot_nt((q * x).astype(BF16), stack_heads((k * x).astype(BF16)))
        o = _dot(a.astype(BF16), v_diag)
        for h in range(GLA_HEADS):
            oh = o[:, h * LANES:(h + 1) * LANES] + inters[0][h * c:(h + 1) * c] + inters[1][h * c:(h + 1) * c]
            of = oh * lax.rsqrt(jnp.mean(oh * oh, axis=-1, keepdims=True) + EPS) * gain_ref[...]
            r = r_ref[h, rows, :].astype(F32)
            o_ref[rows, h * LANES:(h + 1) * LANES] = (of * (r / (1.0 + jnp.exp(-r)))).astype(BF16)
        return carry

    lax.fori_loop(0, chunks, body, 0, unroll=2)


def _gla_output(z, gates, sprev_f, sprev_b, out_m, mask_m, gain, *, batch, seq, tq):
    nt = seq // tq
    chunks = tq // GLA_CHUNK
    kw = GLA_HEADS * GLA_DK
    tile = lambda b, i: (b * nt + i)
    return pl.pallas_call(
        functools.partial(_gla_out_kernel, chunks=chunks),
        grid=(batch, nt),
        in_specs=[pl.BlockSpec((2, tq, LANES), lambda b, i: (ZB_QG // 2, tile(b, i), 0)),
                  pl.BlockSpec((2, tq, LANES), lambda b, i: (ZB_KG // 2, tile(b, i), 0)),
                  pl.BlockSpec((4, tq, LANES), lambda b, i: (ZB_VG // 4, tile(b, i), 0)),
                  pl.BlockSpec((4, tq, LANES), lambda b, i: (ZB_RG // 4, tile(b, i), 0)),
                  pl.BlockSpec((2, tq, kw), lambda b, i: (0, tile(b, i), 0)),
                  pl.BlockSpec((1, chunks, LANES, kw), lambda b, i: (b, i, 0, 0)),
                  pl.BlockSpec((1, chunks, LANES, kw), lambda b, i: (b, i, 0, 0)),
                  pl.BlockSpec(out_m.shape, lambda b, i: (0, 0, 0)),
                  pl.BlockSpec(mask_m.shape, lambda b, i: (0, 0, 0, 0)),
                  pl.BlockSpec((1, LANES), lambda b, i: (0, 0))],
        out_specs=pl.BlockSpec((tq, GLA_HEADS * LANES), lambda b, i: (tile(b, i), 0)),
        out_shape=jax.ShapeDtypeStruct((batch * seq, GLA_HEADS * LANES), BF16),
        compiler_params=_cparams(("parallel", "parallel")),
        name="gla_output",
    )(z, z, z, z, gates, sprev_f, sprev_b, out_m, mask_m, gain.reshape(1, LANES))


def _outproj_kernel(ya_ref, yb_ref, ys_ref, w_ref, x_ref, mod_ref, g_ref, x1_ref, h2_ref):
    na = ya_ref.shape[1]
    nb = yb_ref.shape[1]
    proj = (_dot(ya_ref[...], w_ref[0:na, :]) + _dot(yb_ref[...], w_ref[na:na + nb, :])
            + _dot(ys_ref[...], w_ref[na + nb:, :]))
    x1 = x_ref[...] + mod_ref[0, 2:3, :] * proj
    x1_ref[...] = x1
    ms = jnp.mean(x1 * x1, axis=-1, keepdims=True)
    y = x1 * lax.rsqrt(ms + EPS) * g_ref[...]
    h2_ref[...] = (y * (1.0 + mod_ref[0, 4:5, :]) + mod_ref[0, 3:4, :]).astype(BF16)


def _outproj(ya, yb, ys, w_out, xt, mod, gain, *, tm, tiles_per_batch, ctx_row):
    t, d = xt.shape
    mod_map = (lambda i: (i // tiles_per_batch, 0, 0)) if ctx_row is None else (lambda i: (ctx_row, 0, 0))
    row = lambda i: (i, 0)
    return pl.pallas_call(
        _outproj_kernel,
        grid=(t // tm,),
        in_specs=[pl.BlockSpec((tm, ya.shape[1]), row),
                  pl.BlockSpec((tm, yb.shape[1]), row),
                  pl.BlockSpec((tm, ys.shape[1]), row),
                  pl.BlockSpec(w_out.shape, lambda i: (0, 0), pipeline_mode=pl.Buffered(1)),
                  pl.BlockSpec((tm, d), row),
                  pl.BlockSpec((1, 6, d), mod_map),
                  pl.BlockSpec((1, d), lambda i: (0, 0))],
        out_specs=[pl.BlockSpec((tm, d), row), pl.BlockSpec((tm, d), row)],
        out_shape=[jax.ShapeDtypeStruct((t, d), F32), jax.ShapeDtypeStruct((t, d), BF16)],
        compiler_params=_cparams(("parallel",)),
        name="out_proj",
    )(ya, yb, ys, w_out, xt, mod, gain.reshape(1, d))


def _mlp_kernel(h_ref, w1_ref, w2_ref, x_ref, mod_ref, g_ref, o_ref, acc_ref, *, final_norm):
    f = pl.program_id(1)

    @pl.when(f == 0)
    def _():
        acc_ref[...] = jnp.zeros_like(acc_ref)

    a = jnp.maximum(_dot(h_ref[...], w1_ref[...]), 0.0)
    acc_ref[...] += _dot((a * a).astype(BF16), w2_ref[...])

    @pl.when(f == pl.num_programs(1) - 1)
    def _():
        out = x_ref[...] + mod_ref[0, 5:6, :] * acc_ref[...]
        if final_norm:
            ms = jnp.mean(out * out, axis=-1, keepdims=True)
            out = out * lax.rsqrt(ms + EPS) * g_ref[...]
        o_ref[...] = out


def _mlp(h2, w1, w2, x1, mod, final_gain, *, tm, tf, tiles_per_batch, ctx_row, final_norm):
    t, d = x1.shape
    ff = w1.shape[1]
    mod_map = (lambda i, f: (i // tiles_per_batch, 0, 0)) if ctx_row is None else (lambda i, f: (ctx_row, 0, 0))
    return pl.pallas_call(
        functools.partial(_mlp_kernel, final_norm=final_norm),
        grid=(t // tm, ff // tf),
        in_specs=[pl.BlockSpec((tm, d), lambda i, f: (i, 0)),
                  pl.BlockSpec((d, tf), lambda i, f: (0, f)),
                  pl.BlockSpec((tf, d), lambda i, f: (f, 0)),
                  pl.BlockSpec((tm, d), lambda i, f: (i, 0)),
                  pl.BlockSpec((1, 6, d), mod_map),
                  pl.BlockSpec((1, d), lambda i, f: (0, 0))],
        out_specs=pl.BlockSpec((tm, d), lambda i, f: (i, 0)),
        out_shape=jax.ShapeDtypeStruct((t, d), F32),
        scratch_shapes=[pltpu.VMEM((tm, d), F32)],
        compiler_params=_cparams(("parallel", "arbitrary")),
        name="mlp",
    )(h2, w1, w2, x1, mod, final_gain.reshape(1, d))


def _rope_tables(seq):
    rows = seq // GRID_W
    n_freq = HEAD_DIM // 4
    inv = np.float32(ROPE_THETA) ** (-np.arange(n_freq, dtype=np.float32) / np.float32(n_freq))
    ar = (np.arange(rows, dtype=np.float32)[:, None] * inv[None]).astype(np.float64)
    ac = (np.arange(GRID_W, dtype=np.float32)[:, None] * inv[None]).astype(np.float64)
    cos_r = np.concatenate([np.cos(ar), np.cos(ar)], axis=-1).astype(np.float32)
    sin_r = np.concatenate([-np.sin(ar), np.sin(ar)], axis=-1).astype(np.float32)
    cos_c = np.concatenate([np.cos(ac), np.cos(ac)], axis=-1).astype(np.float32)
    sin_c = np.concatenate([-np.sin(ac), np.sin(ac)], axis=-1).astype(np.float32)

    def table(by_row, by_col):
        r = jnp.broadcast_to(jnp.asarray(by_row)[:, None, :], (rows, GRID_W, 2 * n_freq))
        c = jnp.broadcast_to(jnp.asarray(by_col)[None, :, :], (rows, GRID_W, 2 * n_freq))
        return jnp.concatenate([r, c], axis=-1).reshape(seq, 4 * n_freq)

    return table(cos_r, cos_c), table(sin_r, sin_c)


def _split_w_in(w_in):
    g0 = ZB_QS * LANES
    g1 = g0 + 2 * GLA_GATE_RANK
    ranks = jnp.pad(w_in[:, g0:g1], ((0, 0), (0, LANES - 2 * GLA_GATE_RANK)))
    return w_in[:, :g0].astype(BF16), w_in[:, g1:].astype(BF16), ranks.astype(BF16)


def _gate_weights(wg_f, bg_f, wg_b, bg_b):
    r = GLA_GATE_RANK
    kw = wg_f.shape[1]
    w = jnp.zeros((2, LANES, kw), F32)
    w = w.at[0, 0:r, :].set(wg_f)
    w = w.at[1, r:2 * r, :].set(wg_b)
    b = jnp.stack([bg_f.reshape(1, kw), bg_b.reshape(1, kw)])
    return w.astype(BF16), b.astype(F32)


@jax.jit
def _forward(x, c, ctx, c_ctx, w_mod, b_mod, norm1_g, norm2_g, w_in, na_rpb, gla_wg_fwd, gla_bg_fwd,
             gla_wg_bwd, gla_bg_bwd, gla_norm_g, swa_sink, w_out, w_ff1, w_ff2, final_norm_g):
    batch, seq, d = x.shape
    n_ctx = ctx.shape[1]
    depth = w_mod.shape[0]
    ctx_row = batch
    tm = 512
    tpb = seq // tm

    cond = jnp.concatenate([c, c_ctx[None], jnp.zeros((8 - batch - 1, d), F32)], axis=0)
    mods = _modulation(cond, w_mod, b_mod).reshape(depth, 8, 6, d)
    cos, sin = _rope_tables(seq)
    out_m, scan_m, mask_m = _gla_matrices()
    zero_state = jnp.zeros((batch, 2, LANES, GLA_HEADS * GLA_DK), F32)

    xl = x.reshape(batch * seq, d)
    xc = ctx.reshape(batch * n_ctx, d)
    for i in range(depth):
        with_ctx = i < depth - 1
        mod = mods[i]
        w_in_r = _split_w_in(w_in[i])
        w_out_b = w_out[i].astype(BF16)
        w1 = w_ff1[i].astype(BF16)
        w2 = w_ff2[i].astype(BF16)
        wg, bg = _gate_weights(gla_wg_fwd[i], gla_bg_fwd[i], gla_wg_bwd[i], gla_bg_bwd[i])
        toep = _na_toeplitz(na_rpb[i])

        z, gates_l, kt = _inproj(xl, mod, norm1_g[i], w_in_r, cos, sin, wg, bg, tm=tm, tiles_per_batch=tpb,
                                 ctx_row=None)
        zc, gates_c, ktc = _inproj(xc, mod, norm1_g[i], w_in_r, cos, sin, wg, bg, tm=batch * n_ctx, tiles_per_batch=1,
                                   ctx_row=ctx_row)
        spf_c, spb_c, sfin_c = _gla_scan(zc, gates_c, scan_m, zero_state, batch=batch, seq=n_ctx, tq=n_ctx)
        spf_l, spb_l, _ = _gla_scan(z, gates_l, scan_m, sfin_c, batch=batch, seq=seq, tq=min(2048, seq))

        ya = _neighbourhood(z, zc, toep, batch=batch, seq=seq, n_ctx=n_ctx)
        yb = _gla_output(z, gates_l, spf_l, spb_l, out_m, mask_m, gla_norm_g[i], batch=batch, seq=seq, tq=512)
        ys = _window_attention(z, zc, kt, ktc, swa_sink[i], batch=batch, seq=seq, n_ctx=n_ctx, tq=min(1024, seq))

        x1, h2 = _outproj(ya, yb, ys, w_out_b, xl, mod, norm2_g[i], tm=tm, tiles_per_batch=tpb, ctx_row=None)
        xl = _mlp(h2, w1, w2, x1, mod, final_norm_g, tm=tm, tf=1024, tiles_per_batch=tpb, ctx_row=None,
                  final_norm=not with_ctx)

        if with_ctx:
            ya_c = _ctx_attention(zc, swa_sink[i], batch=batch, n_ctx=n_ctx, q_blk=ZB_QA, k_blk=ZB_KA,
                                  v_blk=ZB_VA, heads=NA_HEADS, group=1, use_sink=False)
            yb_c = _gla_output(zc, gates_c, spf_c, spb_c, out_m, mask_m, gla_norm_g[i], batch=batch, seq=n_ctx,
                               tq=n_ctx)
            ys_c = _ctx_attention(zc, swa_sink[i], batch=batch, n_ctx=n_ctx, q_blk=ZB_QS, k_blk=ZB_KS,
                                  v_blk=ZB_VS, heads=SWA_Q_HEADS, group=SWA_GROUP, use_sink=True)
            x1c, h2c = _outproj(ya_c, yb_c, ys_c, w_out_b, xc, mod, norm2_g[i], tm=batch * n_ctx, tiles_per_batch=1,
                                ctx_row=ctx_row)
            xc = _mlp(h2c, w1, w2, x1c, mod, final_norm_g, tm=batch * n_ctx, tf=1024, tiles_per_batch=1,
                      ctx_row=ctx_row, final_norm=False)
    return xl.reshape(batch, seq, d)


def kernel(x, c, ctx, c_ctx, w_mod, b_mod, norm1_g, norm2_g, w_in, na_rpb, gla_wg_fwd, gla_bg_fwd, gla_wg_bwd,
           gla_bg_bwd, gla_norm_g, swa_sink, w_out, w_ff1, w_ff2, final_norm_g):
    return _forward(x, c, ctx, c_ctx, w_mod, b_mod, norm1_g, norm2_g, w_in, na_rpb, gla_wg_fwd, gla_bg_fwd,
                    gla_wg_bwd, gla_bg_bwd, gla_norm_g, swa_sink, w_out, w_ff1, w_ff2, final_norm_g)
```
